```python
import math
import jax
import jax.numpy as jnp
from jax import lax
import numpy as np


D_MODEL = 2048
BATCH = 4
SEQ = 2048
DEPTH = 2
DEC_BATCH = 32
DEC_SEQ = 8
PAST_LEN = 8192
PAGE_SIZE = 128

N_EVEN = (DEPTH + 1) // 2
N_ODD = DEPTH // 2
D_RNN = D_MODEL // 2
RG_HEADS = 8
RG_HEAD_DIM = D_RNN // RG_HEADS
RG_CONV = 4
RG_C = 8.0
N_HEADS = 16
HEAD_DIM = 64
N_KV = 4
HPG = N_HEADS // N_KV
D_ATTN = N_HEADS * HEAD_DIM
D_KV = N_KV * HEAD_DIM
BLOCK = 64
TOPK = 16
WINDOW = 512
N_BRANCH = 3
Q_BLOCK = 128
SCALE = HEAD_DIM ** -0.5
FORCE = 1e4
NEG = -1e30
N_BUCKETS = 32
MAX_DIST = 128
D_CONV = D_MODEL
SC_WIDTH = 3
D_FF = 4 * D_MODEL
EPS = 1e-6
D_IN_E = 2 * D_RNN + D_ATTN + 6 * D_KV + N_BRANCH * N_HEADS
D_OUT_E = D_RNN + D_ATTN
F32 = jnp.float32

kernel_name = 'nsa_rglru_shortconv_hybrid_step'


def _rmsnorm(x, g):
    xf = x.astype(F32)
    y = xf * lax.rsqrt(jnp.mean(xf * xf, axis=-1, keepdims=True) + EPS)
    return (y * g.astype(F32)).astype(x.dtype)


def _t5_bucket(dist):
    n = jnp.maximum(dist, 0)
    max_exact = N_BUCKETS // 2
    nf = jnp.maximum(n, 1).astype(F32)
    large = max_exact + (jnp.log(nf / max_exact) / math.log(MAX_DIST / max_exact) * (N_BUCKETS - max_exact)).astype(jnp.int32)
    large = jnp.minimum(large, N_BUCKETS - 1)
    return jnp.where(n < max_exact, n, large)


def _causal_dwconv(x, buf, w):
    width = w.shape[0]
    t = x.shape[1]
    xx = jnp.concatenate([buf.astype(x.dtype), x], axis=1)
    y = xx[:, 0:t] * w[0]
    for j in range(1, width):
        y = y + xx[:, j:j + t] * w[j]
    return y, xx[:, t:]


def _rglru(x, h0, W):
    b_, t, _ = x.shape
    xf = x.astype(F32)
    xh = xf.reshape(b_, t, RG_HEADS, RG_HEAD_DIM)
    r = jax.nn.sigmoid(jnp.einsum('bthi,hij->bthj', xh, W['rg_wa'].astype(F32)).reshape(b_, t, D_RNN) + W['rg_ba'].astype(F32))
    i = jax.nn.sigmoid(jnp.einsum('bthi,hij->bthj', xh, W['rg_wx'].astype(F32)).reshape(b_, t, D_RNN) + W['rg_bx'].astype(F32))
    log_a = -RG_C * r * jax.nn.softplus(-W['rg_lambda'].astype(F32))
    a = jnp.exp(log_a)
    u = jnp.sqrt(-jnp.expm1(2.0 * log_a)) * (i * xf)

    def step(hc, au):
        hc = au[0] * hc + au[1]
        return hc, hc

    h_last, hs = lax.scan(step, h0.astype(F32), (jnp.swapaxes(a, 0, 1), jnp.swapaxes(u, 0, 1)))
    return jnp.swapaxes(hs, 0, 1).astype(x.dtype), h_last.astype(x.dtype)


def _split_even(z):
    sizes = [D_RNN, D_RNN, D_ATTN] + [D_KV] * 6 + [N_BRANCH * N_HEADS]
    cuts = [int(c) for c in np.cumsum(sizes)[:-1]]
    return jnp.split(z, cuts, axis=-1)


def _even_project(h, W):
    b_, t = h.shape[:2]
    z = jnp.einsum('btd,de->bte', h, W['w_in'])
    rx, rg, q, kc, vc, ks, vs, kw, vw, gl = _split_even(z)
    q = _rmsnorm(q.reshape(b_, t, N_HEADS, HEAD_DIM), W['q_norm']).reshape(b_, t, N_KV, HPG, HEAD_DIM)
    kc = kc.reshape(b_, t, N_KV, HEAD_DIM)
    vc = vc.reshape(b_, t, N_KV, HEAD_DIM)
    ks = _rmsnorm(ks.reshape(b_, t, N_KV, HEAD_DIM), W['k_norm'][1])
    vs = vs.reshape(b_, t, N_KV, HEAD_DIM)
    kw = _rmsnorm(kw.reshape(b_, t, N_KV, HEAD_DIM), W['k_norm'][2])
    vw = vw.reshape(b_, t, N_KV, HEAD_DIM)
    gates = jax.nn.sigmoid(gl.astype(F32)).reshape(b_, t, N_BRANCH, N_KV, HPG, 1).astype(h.dtype)
    return rx, rg, q, kc, vc, ks, vs, kw, vw, gates


def _rg_branch(rx, rg, conv_buf, h0, W):
    xc, new_buf = _causal_dwconv(rx, conv_buf, W['rg_conv_w'])
    y, h_last = _rglru(xc + W['rg_conv_b'], h0, W)
    return jax.nn.gelu(rg) * y, new_buf, h_last


def _compress(k, pe, w1, w2):
    b_, t = k.shape[:2]
    kb = k.reshape(b_, t // BLOCK, BLOCK, N_KV, HEAD_DIM) + pe
    hid = jax.nn.gelu(jnp.einsum('bnjgd,jde->bnge', kb, w1))
    return jnp.einsum('bnge,ef->bngf', hid, w2)


def _dense_core(q, k, v, dist, valid, tab):
    t, m = dist.shape
    s = jnp.einsum('btgnd,bmgd->btgnm', q, k).astype(F32) * SCALE
    bias = tab[_t5_bucket(dist)].astype(F32).reshape(t, m, N_KV, HPG).transpose(0, 2, 3, 1)
    msk = valid[:, None, None, :]
    s = jnp.where(msk, s + bias, NEG)
    p = jnp.where(msk, jax.nn.softmax(s, axis=-1), 0.0)
    o = jnp.einsum('btgnm,bmgd->btgnd', p.astype(v.dtype), v)
    return o, p


def _select_blocks(p, qpos, n_sel):
    nb = p.shape[-1]
    imp = p.sum(axis=3)
    cur = (qpos // BLOCK)[:, None]
    blk = jnp.arange(nb)[None, :]
    forced = (blk == cur) | (blk == cur - 1) | (blk == 0)
    imp = imp + jnp.where(forced, FORCE, 0.0)[None, :, None, :]
    imp = jnp.where((blk > cur)[None, :, None, :], NEG, imp)
    return lax.top_k(imp, n_sel)[1]


def _sel_core(q, kg, vg, tok, qpos, tab):
    s = jnp.einsum('tgnd,tgld->tgnl', q, kg).astype(F32) * SCALE
    dist = qpos[:, None, None] - tok
    tab_g = tab.reshape(N_BUCKETS, N_KV, HPG)
    bias = tab_g[_t5_bucket(dist), jnp.arange(N_KV)[None, :, None]].astype(F32).transpose(0, 1, 3, 2)
    msk = (dist >= 0)[:, :, None, :]
    s = jnp.where(msk, s + bias, NEG)
    p = jnp.where(msk, jax.nn.softmax(s, axis=-1), 0.0)
    return jnp.einsum('tgnl,tgld->tgnd', p.astype(vg.dtype), vg)


def _block_tokens(ii):
    t = ii.shape[0]
    return (ii[..., None] * BLOCK + jnp.arange(BLOCK)).reshape(t, N_KV, -1)


def _sel_prompt(q, k, v, idx, tab):
    b_, s_ = q.shape[:2]
    nqb = s_ // Q_BLOCK
    n_sel = idx.shape[-1]
    qb = q.reshape(b_ * nqb, Q_BLOCK, N_KV, HPG, HEAD_DIM)
    ib = idx.reshape(b_ * nqb, Q_BLOCK, N_KV, n_sel)
    bid = jnp.repeat(jnp.arange(b_), nqb)
    start = jnp.tile(jnp.arange(nqb) * Q_BLOCK, b_)
    garr = jnp.arange(N_KV)[None, :, None]

    def one(args):
        qi, ii, b, s0 = args
        tok = _block_tokens(ii)
        return _sel_core(qi, k[b, tok, garr], v[b, tok, garr], tok, s0 + jnp.arange(Q_BLOCK), tab)

    o = lax.map(one, (qb, ib, bid, start))
    return o.reshape(b_, s_, N_KV, HPG, HEAD_DIM)


def _sel_sample(q, pool_k, pool_v, new_k, new_v, page_table, idx, qpos, tab):
    t = q.shape[1]
    past = page_table.shape[1] * PAGE_SIZE
    garr = jnp.arange(N_KV)[None, :, None]

    def one(args):
        qi, ii, pt, nk, nv = args
        tok = _block_tokens(ii)
        tp = jnp.minimum(tok, past - 1)
        phys = pt[tp // PAGE_SIZE]
        slot = tp % PAGE_SIZE
        tn = jnp.clip(tok - past, 0, t - 1)
        in_past = (tok < past)[..., None]
        kg = jnp.where(in_past, pool_k[phys, slot, garr], nk[tn, garr])
        vg = jnp.where(in_past, pool_v[phys, slot, garr], nv[tn, garr])
        return _sel_core(qi, kg, vg, tok, qpos, tab)

    return lax.map(one, (q, idx, page_table, new_k, new_v))


def _win_prompt(q, k, v, tab):
    b_, s_ = q.shape[:2]
    nqb = s_ // Q_BLOCK
    span = WINDOW + Q_BLOCK
    pad = ((0, 0), (WINDOW, 0), (0, 0), (0, 0))
    kp = jnp.pad(k, pad)
    vp = jnp.pad(v, pad)
    qb = jnp.swapaxes(q.reshape(b_, nqb, Q_BLOCK, N_KV, HPG, HEAD_DIM), 0, 1)

    def one(args):
        qi, i = args
        s0 = i * Q_BLOCK
        kb = lax.dynamic_slice_in_dim(kp, s0, span, axis=1)
        vb = lax.dynamic_slice_in_dim(vp, s0, span, axis=1)
        qpos = s0 + jnp.arange(Q_BLOCK)
        kpos = s0 - WINDOW + jnp.arange(span)
        dist = qpos[:, None] - kpos[None, :]
        valid = (dist >= 0) & (dist <= WINDOW) & (kpos[None, :] >= 0)
        return _dense_core(qi, kb, vb, dist, valid, tab)[0]

    o = lax.map(one, (qb, jnp.arange(nqb)))
    return jnp.swapaxes(o, 0, 1).reshape(b_, s_, N_KV, HPG, HEAD_DIM)


def _even_out(y_rnn, o_c, o_s, o_w, gates, W):
    b_, t = y_rnn.shape[:2]
    o = gates[:, :, 0] * o_c + gates[:, :, 1] * o_s + gates[:, :, 2] * o_w
    cat = jnp.concatenate([y_rnn, o.reshape(b_, t, D_ATTN)], axis=-1)
    return jnp.einsum('bte,ed->btd', cat, W['w_out'])


def _even_prompt(h, W, tab):
    b_, s_ = h.shape[:2]
    rx, rg, q, kc, vc, ks, vs, kw, vw, gates = _even_project(h, W)
    y_rnn, cbuf, h_last = _rg_branch(rx, rg, jnp.zeros((b_, RG_CONV - 1, D_RNN), h.dtype), jnp.zeros((b_, D_RNN), F32), W)
    qpos = jnp.arange(s_)
    kcb = _rmsnorm(_compress(kc, W['pe_k'], W['w1_k'], W['w2_k']), W['k_norm'][0])
    vcb = _compress(vc, W['pe_v'], W['w1_v'], W['w2_v'])
    nb = kcb.shape[1]
    dist = qpos[:, None] - (jnp.arange(nb) * BLOCK + (BLOCK - 1))[None, :]
    o_c, p_c = _dense_core(q, kcb, vcb, dist, dist >= 0, tab)
    idx = _select_blocks(p_c, qpos, min(TOPK, nb))
    o_s = _sel_prompt(q, ks, vs, idx, tab)
    o_w = _win_prompt(q, kw, vw, tab)
    out = _even_out(y_rnn, o_c, o_s, o_w, gates, W)
    keep = min(WINDOW, s_)
    return out, (kc, vc, ks, vs, kw[:, s_ - keep:], vw[:, s_ - keep:], h_last, cbuf)


def _even_sample(h, W, tab, pool_ck, pool_cv, pool_sk, pool_sv, win_k, win_v, h_state, conv_state, page_table):
    b_, t = h.shape[:2]
    past = page_table.shape[1] * PAGE_SIZE
    rx, rg, q, kc, vc, ks, vs, kw, vw, gates = _even_project(h, W)
    y_rnn, cbuf, h_last = _rg_branch(rx, rg, conv_state, h_state, W)
    qpos = past + jnp.arange(t)
    nb = -(-(past + t) // BLOCK)
    padw = ((0, 0), (0, nb * BLOCK - past - t), (0, 0), (0, 0))
    kc_all = jnp.pad(jnp.concatenate([pool_ck[page_table].reshape(b_, past, N_KV, HEAD_DIM), kc], axis=1), padw)
    vc_all = jnp.pad(jnp.concatenate([pool_cv[page_table].reshape(b_, past, N_KV, HEAD_DIM), vc], axis=1), padw)
    kcb = _rmsnorm(_compress(kc_all, W['pe_k'], W['w1_k'], W['w2_k']), W['k_norm'][0])
    vcb = _compress(vc_all, W['pe_v'], W['w1_v'], W['w2_v'])
    dist = qpos[:, None] - (jnp.arange(nb) * BLOCK + (BLOCK - 1))[None, :]
    o_c, p_c = _dense_core(q, kcb, vcb, dist, dist >= 0, tab)
    idx = _select_blocks(p_c, qpos, min(TOPK, nb))
    o_s = _sel_sample(q, pool_sk, pool_sv, ks, vs, page_table, idx, qpos, tab)
    wb = win_k.shape[1]
    wk = jnp.concatenate([win_k, kw], axis=1)
    wv = jnp.concatenate([win_v, vw], axis=1)
    kpos = past - wb + jnp.arange(wb + t)
    dist_w = qpos[:, None] - kpos[None, :]
    valid_w = (dist_w >= 0) & (dist_w <= WINDOW) & (kpos[None, :] >= 0)
    o_w = _dense_core(q, wk, wv, dist_w, valid_w, tab)[0]
    out = _even_out(y_rnn, o_c, o_s, o_w, gates, W)
    keep = min(WINDOW, past + t)
    return out, (kc, vc, ks, vs, wk[:, wb + t - keep:], wv[:, wb + t - keep:], h_last, cbuf)


def _shortconv_mixer(h, buf, w_in, w_conv, w_out):
    z = jnp.einsum('btd,de->bte', h, w_in)
    bg, cg, u = jnp.split(z, 3, axis=-1)
    y, new_buf = _causal_dwconv(cg * u, buf, w_conv)
    return jnp.einsum('btc,cd->btd', bg * y, w_out), new_buf


def _ffn(h, w1, w2):
    a = jax.nn.relu(jnp.einsum('btd,df->btf', h, w1))
    return jnp.einsum('btf,fd->btd', a * a, w2)


def setup_inputs(seed: int = 0) -> dict:
    key = jax.random.key(seed)
    keys = iter(jax.random.split(key, 48))

    def nrm(shape, scale):
        return jax.random.normal(next(keys), shape, F32) * scale

    n_pages = PAST_LEN // PAGE_SIZE
    n_used = DEC_BATCH * n_pages
    n_pool = n_used + max(1, n_used // 4)
    wb = min(WINDOW, PAST_LEN)
    pool_shape = (N_EVEN, n_pool, PAGE_SIZE, N_KV, HEAD_DIM)
    win_shape = (N_EVEN, DEC_BATCH, wb, N_KV, HEAD_DIM)
    x_prompt = nrm((BATCH, SEQ, D_MODEL), 1.0)
    x_sample = nrm((DEC_BATCH, DEC_SEQ, D_MODEL), 1.0)
    cache_cmp_k = nrm(pool_shape, 1.0)
    cache_cmp_v = nrm(pool_shape, 1.0)
    cache_sel_k = nrm(pool_shape, 1.0)
    cache_sel_v = nrm(pool_shape, 1.0)
    cache_win_k = nrm(win_shape, 1.0)
    cache_win_v = nrm(win_shape, 1.0)
    state_rglru_h = nrm((N_EVEN, DEC_BATCH, D_RNN), 0.5)
    state_rglru_conv = nrm((N_EVEN, DEC_BATCH, RG_CONV - 1, D_RNN), 1.0)
    state_sconv = nrm((N_ODD, DEC_BATCH, SC_WIDTH - 1, D_CONV), 1.0)
    page_table = jax.random.permutation(next(keys), n_pool)[:n_used].reshape(DEC_BATCH, n_pages).astype(jnp.int32)
    a_c = jax.random.uniform(next(keys), (N_EVEN, D_RNN), F32, minval=0.9, maxval=0.999)
    s_l = a_c ** (1.0 / RG_C)
    rg_lambda = jnp.log(s_l) - jnp.log1p(-s_l)
    return {
        'x_prompt': x_prompt,
        'x_sample': x_sample,
        'cache_cmp_k': cache_cmp_k,
        'cache_cmp_v': cache_cmp_v,
        'cache_sel_k': cache_sel_k,
        'cache_sel_v': cache_sel_v,
        'cache_win_k': cache_win_k,
        'cache_win_v': cache_win_v,
        'state_rglru_h': state_rglru_h,
        'state_rglru_conv': state_rglru_conv,
        'state_sconv': state_sconv,
        'page_table': page_table,
        'rel_bias': nrm((N_BUCKETS, N_HEADS), 0.5),
        'norm_mix': 1.0 + nrm((DEPTH, D_MODEL), 0.02),
        'norm_ffn': 1.0 + nrm((DEPTH, D_MODEL), 0.02),
        'w_ff1': nrm((DEPTH, D_MODEL, D_FF), D_MODEL ** -0.5),
        'w_ff2': nrm((DEPTH, D_FF, D_MODEL), D_FF ** -0.5),
        'w_in_e': nrm((N_EVEN, D_MODEL, D_IN_E), D_MODEL ** -0.5),
        'w_out_e': nrm((N_EVEN, D_OUT_E, D_MODEL), D_OUT_E ** -0.5),
        'rg_conv_w': nrm((N_EVEN, RG_CONV, D_RNN), 0.5),
        'rg_conv_b': nrm((N_EVEN, D_RNN), 0.05),
        'rg_wa': nrm((N_EVEN, RG_HEADS, RG_HEAD_DIM, RG_HEAD_DIM), RG_HEAD_DIM ** -0.5),
        'rg_ba': nrm((N_EVEN, D_RNN), 0.1),
        'rg_wx': nrm((N_EVEN, RG_HEADS, RG_HEAD_DIM, RG_HEAD_DIM), RG_HEAD_DIM ** -0.5),
        'rg_bx': nrm((N_EVEN, D_RNN), 0.1),
        'rg_lambda': rg_lambda,
        'q_norm': 1.0 + nrm((N_EVEN, HEAD_DIM), 0.02),
        'k_norm': 1.0 + nrm((N_EVEN, N_BRANCH, HEAD_DIM), 0.02),
        'cmp_pe_k': nrm((N_EVEN, BLOCK, N_KV, HEAD_DIM), 0.5),
        'cmp_pe_v': nrm((N_EVEN, BLOCK, N_KV, HEAD_DIM), 0.5),
        'cmp_w1_k': nrm((N_EVEN, BLOCK, HEAD_DIM, HEAD_DIM), (BLOCK * HEAD_DIM) ** -0.5),
        'cmp_w2_k': nrm((N_EVEN, HEAD_DIM, HEAD_DIM), HEAD_DIM ** -0.5),
        'cmp_w1_v': nrm((N_EVEN, BLOCK, HEAD_DIM, HEAD_DIM), (BLOCK * HEAD_DIM) ** -0.5),
        'cmp_w2_v': nrm((N_EVEN, HEAD_DIM, HEAD_DIM), HEAD_DIM ** -0.5),
        'w_in_o': nrm((N_ODD, D_MODEL, 3 * D_CONV), D_MODEL ** -0.5),
        'sc_w': nrm((N_ODD, SC_WIDTH, D_CONV), SC_WIDTH ** -0.5),
        'w_out_o': nrm((N_ODD, D_CONV, D_MODEL), D_CONV ** -0.5),
    }


def reference(x_prompt, x_sample, cache_cmp_k, cache_cmp_v, cache_sel_k, cache_sel_v, cache_win_k, cache_win_v,
              state_rglru_h, state_rglru_conv, state_sconv, page_table, rel_bias, norm_mix, norm_ffn, w_ff1, w_ff2,
              w_in_e, w_out_e, rg_conv_w, rg_conv_b, rg_wa, rg_ba, rg_wx, rg_bx, rg_lambda, q_norm, k_norm,
              cmp_pe_k, cmp_pe_v, cmp_w1_k, cmp_w2_k, cmp_w1_v, cmp_w2_v, w_in_o, sc_w, w_out_o):
    y_p = x_prompt
    y_s = x_sample
    ev_p, ev_s, od_p, od_s = [], [], [], []
    for li in range(DEPTH):
        j = li // 2
        h_p = _rmsnorm(y_p, norm_mix[li])
        h_s = _rmsnorm(y_s, norm_mix[li])
        if li % 2 == 0:
            W = {'w_in': w_in_e[j], 'w_out': w_out_e[j], 'rg_conv_w': rg_conv_w[j], 'rg_conv_b': rg_conv_b[j],
                 'rg_wa': rg_wa[j], 'rg_ba': rg_ba[j], 'rg_wx': rg_wx[j], 'rg_bx': rg_bx[j], 'rg_lambda': rg_lambda[j],
                 'q_norm': q_norm[j], 'k_norm': k_norm[j], 'pe_k': cmp_pe_k[j], 'pe_v': cmp_pe_v[j],
                 'w1_k': cmp_w1_k[j], 'w2_k': cmp_w2_k[j], 'w1_v': cmp_w1_v[j], 'w2_v': cmp_w2_v[j]}
            m_p, st_p = _even_prompt(h_p, W, rel_bias)
            m_s, st_s = _even_sample(h_s, W, rel_bias, cache_cmp_k[j], cache_cmp_v[j], cache_sel_k[j], cache_sel_v[j],
                                     cache_win_k[j], cache_win_v[j], state_rglru_h[j], state_rglru_conv[j], page_table)
            ev_p.append(st_p)
            ev_s.append(st_s)
        else:
            m_p, b_p = _shortconv_mixer(h_p, jnp.zeros((h_p.shape[0], SC_WIDTH - 1, D_CONV), h_p.dtype), w_in_o[j], sc_w[j], w_out_o[j])
            m_s, b_s = _shortconv_mixer(h_s, state_sconv[j], w_in_o[j], sc_w[j], w_out_o[j])
            od_p.append(b_p)
            od_s.append(b_s)
        y_p = y_p + m_p
        y_s = y_s + m_s
        y_p = y_p + _ffn(_rmsnorm(y_p, norm_ffn[li]), w_ff1[li], w_ff2[li])
        y_s = y_s + _ffn(_rmsnorm(y_s, norm_ffn[li]), w_ff1[li], w_ff2[li])

    def _stk(lst, i):
        return jnp.stack([st[i] for st in lst])

    cmp_k_p = _stk(ev_p, 0)
    cmp_k_s = _stk(ev_s, 0)
    cmp_v_p = _stk(ev_p, 1)
    cmp_v_s = _stk(ev_s, 1)
    sel_k_p = _stk(ev_p, 2)
    sel_k_s = _stk(ev_s, 2)
    sel_v_p = _stk(ev_p, 3)
    sel_v_s = _stk(ev_s, 3)
    win_k_p = _stk(ev_p, 4)
    win_k_s = _stk(ev_s, 4)
    win_v_p = _stk(ev_p, 5)
    win_v_s = _stk(ev_s, 5)
    rglru_h_p = _stk(ev_p, 6)
    rglru_h_s = _stk(ev_s, 6)
    rglru_conv_p = _stk(ev_p, 7)
    rglru_conv_s = _stk(ev_s, 7)
    sconv_p = jnp.stack(od_p)
    sconv_s = jnp.stack(od_s)
    return (y_p, y_s, cmp_k_p, cmp_k_s, cmp_v_p, cmp_v_s, sel_k_p, sel_k_s, sel_v_p, sel_v_s,
            win_k_p, win_k_s, win_v_p, win_v_s, rglru_h_p, rglru_h_s, rglru_conv_p, rglru_conv_s, sconv_p, sconv_s)
```

```python
import functools
import math

import numpy as np
import jax
import jax.numpy as jnp
from jax import lax
from jax.experimental import pallas as pl
from jax.experimental.pallas import tpu as pltpu

F32 = jnp.float32
BF16 = jnp.bfloat16

D_MODEL = 2048
BATCH = 4
SEQ = 2048
DEC_BATCH = 32
DEC_SEQ = 8
PAST_LEN = 8192
PAGE_SIZE = 128
D_RNN = 1024
RG_HEADS = 8
RG_HEAD_DIM = 128
RG_C = 8.0
N_HEADS = 16
HEAD_DIM = 64
N_KV = 4
HPG = 4
D_ATTN = 1024
D_KV = 256
BLOCK = 64
TOPK = 16
WINDOW = 512
SCALE = HEAD_DIM ** -0.5
FORCE = 1e4
NEG = -1e30
N_BUCKETS = 32
MAX_DIST = 128
D_CONV = 2048
D_FF = 8192
EPS = 1e-6
D_IN_E = 4656
D_IN_E_PAD = 4864

N_PAGES = PAST_LEN // PAGE_SIZE
NB_P = SEQ // BLOCK
NB_S = PAST_LEN // BLOCK + 1
NB_S_PAD = 136
QT = 128
N_QT = SEQ // QT

LANES = 128
VMEM_LIMIT = 52 * 1024 * 1024

_NT = (((1,), (1,)), ((), ()))
_TN = (((0,), (0,)), ((), ()))


def _cparams(n_axes, vmem=VMEM_LIMIT):
    return pltpu.CompilerParams(dimension_semantics=("arbitrary",) * n_axes, vmem_limit_bytes=vmem)


def _gelu(x):
    return 0.5 * x * (1.0 + jnp.tanh(0.7978845608028654 * (x + 0.044715 * (x * x * x))))


def _sigmoid(x):
    return 1.0 / (1.0 + jnp.exp(-x))


def _rms_rows(x, g):
    ms = jnp.mean(x * x, axis=-1, keepdims=True)
    return (x * lax.rsqrt(ms + EPS)) * g


def _norm_matmul_kernel(x_ref, g_ref, w_ref, o_ref, xn_ref):
    @pl.when(pl.program_id(1) == 0)
    def _():
        xn_ref[...] = _rms_rows(x_ref[...], g_ref[...]).astype(BF16)

    o_ref[...] = jnp.dot(xn_ref[...], w_ref[...], preferred_element_type=F32)


def _norm_matmul(x, g, w, tm, tn):
    m, d = x.shape
    n = w.shape[1]
    return pl.pallas_call(
        _norm_matmul_kernel,
        grid=(m // tm, n // tn),
        in_specs=[pl.BlockSpec((tm, d), lambda i, j: (i, 0)),
                  pl.BlockSpec((1, d), lambda i, j: (0, 0)),
                  pl.BlockSpec((d, tn), lambda i, j: (0, j))],
        out_specs=pl.BlockSpec((tm, tn), lambda i, j: (i, j)),
        out_shape=jax.ShapeDtypeStruct((m, n), F32),
        scratch_shapes=[pltpu.VMEM((tm, d), BF16)],
        compiler_params=_cparams(2),
        name="norm_matmul",
    )(x, g.reshape(1, d), w)


def _ffn_kernel(x_ref, g_ref, w1_ref, w2_ref, o_ref, xn_ref):
    j = pl.program_id(1)

    @pl.when(j == 0)
    def _():
        x = x_ref[...]
        xn_ref[...] = _rms_rows(x, g_ref[...]).astype(BF16)
        o_ref[...] = x

    h = jnp.dot(xn_ref[...], w1_ref[...], preferred_element_type=F32)
    a = jnp.maximum(h, 0.0)
    a = (a * a).astype(BF16)
    o_ref[...] += jnp.dot(a, w2_ref[...], preferred_element_type=F32)


def _ffn(x, g, w1, w2, tm, tf):
    m, d = x.shape
    f = w1.shape[1]
    return pl.pallas_call(
        _ffn_kernel,
        grid=(m // tm, f // tf),
        in_specs=[pl.BlockSpec((tm, d), lambda i, j: (i, 0)),
                  pl.BlockSpec((1, d), lambda i, j: (0, 0)),
                  pl.BlockSpec((d, tf), lambda i, j: (0, j)),
                  pl.BlockSpec((tf, d), lambda i, j: (j, 0))],
        out_specs=pl.BlockSpec((tm, d), lambda i, j: (i, 0)),
        out_shape=jax.ShapeDtypeStruct((m, d), F32),
        scratch_shapes=[pltpu.VMEM((tm, d), BF16)],
        compiler_params=_cparams(2),
        name="ffn",
    )(x, g.reshape(1, d), w1, w2)


def _proj_res_kernel(x_ref, a1_ref, a2_ref, w1_ref, w2_ref, o_ref):
    acc = jnp.dot(a1_ref[...], w1_ref[...], preferred_element_type=F32)
    acc = acc + jnp.dot(a2_ref[...], w2_ref[...], preferred_element_type=F32)
    o_ref[...] = x_ref[...] + acc


def _proj_res(x, a1, a1_col, a2, a2_col, w, tm):
    m, d = x.shape
    kh = w.shape[0] // 2
    return pl.pallas_call(
        _proj_res_kernel,
        grid=(m // tm,),
        in_specs=[pl.BlockSpec((tm, d), lambda i: (i, 0)),
                  pl.BlockSpec((tm, kh), lambda i: (i, a1_col)),
                  pl.BlockSpec((tm, kh), lambda i: (i, a2_col)),
                  pl.BlockSpec((kh, d), lambda i: (0, 0)),
                  pl.BlockSpec((kh, d), lambda i: (1, 0))],
        out_specs=pl.BlockSpec((tm, d), lambda i: (i, 0)),
        out_shape=jax.ShapeDtypeStruct((m, d), F32),
        compiler_params=_cparams(1),
        name="proj_res",
    )(x, a1, a2, w, w)


def _group_sumsq(x, s):
    x2 = x * x
    hi = x2.astype(BF16)
    lo = (x2 - hi.astype(F32)).astype(BF16)
    return jnp.dot(hi, s, preferred_element_type=F32) + jnp.dot(lo, s, preferred_element_type=F32)


def _head_norm(x, gain, s):
    outs = []
    for c in range(x.shape[1] // LANES):
        xc = x[:, c * LANES:(c + 1) * LANES]
        ss = _group_sumsq(xc, s)
        outs.append(xc * lax.rsqrt(ss * (1.0 / HEAD_DIM) + EPS))
    return jnp.concatenate(outs, axis=1) * gain


def _prep0_kernel(q_ref, ks_ref, kw_ref, gl_ref, qg_ref, ksg_ref, kwg_ref, s_ref,
                  qn_ref, ksn_ref, kwn_ref, gt_ref):
    s = s_ref[...]
    qn_ref[...] = (_head_norm(q_ref[...], qg_ref[...], s) * SCALE).astype(BF16)
    ksn_ref[...] = _head_norm(ks_ref[...], ksg_ref[...], s)
    kwn_ref[...] = _head_norm(kw_ref[...], kwg_ref[...], s)
    gt_ref[...] = _sigmoid(gl_ref[...])


def _prep0(z, q_gain, ks_gain, kw_gain, s128, tm):
    m = z.shape[0]
    return pl.pallas_call(
        _prep0_kernel,
        grid=(m // tm,),
        in_specs=[pl.BlockSpec((tm, D_ATTN), lambda i: (i, 2)),
                  pl.BlockSpec((tm, D_KV), lambda i: (i, 14)),
                  pl.BlockSpec((tm, D_KV), lambda i: (i, 16)),
                  pl.BlockSpec((tm, LANES), lambda i: (i, 36)),
                  pl.BlockSpec((1, D_ATTN), lambda i: (0, 0)),
                  pl.BlockSpec((1, D_KV), lambda i: (0, 0)),
                  pl.BlockSpec((1, D_KV), lambda i: (0, 0)),
                  pl.BlockSpec((LANES, LANES), lambda i: (0, 0))],
        out_specs=[pl.BlockSpec((tm, D_ATTN), lambda i: (i, 0)),
                   pl.BlockSpec((tm, D_KV), lambda i: (i, 0)),
                   pl.BlockSpec((tm, D_KV), lambda i: (i, 0)),
                   pl.BlockSpec((tm, LANES), lambda i: (i, 0))],
        out_shape=[jax.ShapeDtypeStruct((m, D_ATTN), BF16),
                   jax.ShapeDtypeStruct((m, D_KV), F32),
                   jax.ShapeDtypeStruct((m, D_KV), F32),
                   jax.ShapeDtypeStruct((m, LANES), F32)],
        compiler_params=_cparams(1),
        name="prep0",
    )(z, z, z, z, q_gain, ks_gain, kw_gain, s128)


def _rglru_kernel(rx_ref, rg_ref, halo_ref, cinit_ref, h0_ref, cw_ref, cb_ref, wa_ref, ba_ref,
                  wx_ref, bx_ref, lam_ref, y_ref, hl_ref, ext_ref, a_ref, u_ref, hs_ref, hc_ref, *, tc):
    c = pl.program_id(1)

    @pl.when(c == 0)
    def _():
        ext_ref[0:8, :] = cinit_ref[0]
        hc_ref[...] = h0_ref[0]

    @pl.when(c > 0)
    def _():
        ext_ref[0:8, :] = halo_ref[...]

    ext_ref[8:8 + tc, :] = rx_ref[...]
    w = cw_ref[...]
    xc = ext_ref[pl.ds(5, tc), :] * w[0:1, :]
    xc = xc + ext_ref[pl.ds(6, tc), :] * w[1:2, :]
    xc = xc + ext_ref[pl.ds(7, tc), :] * w[2:3, :]
    xc = xc + ext_ref[pl.ds(8, tc), :] * w[3:4, :]
    xc = xc + cb_ref[...]

    ras, ias = [], []
    for h in range(RG_HEADS):
        xh = xc[:, h * RG_HEAD_DIM:(h + 1) * RG_HEAD_DIM].astype(BF16)
        ras.append(jnp.dot(xh, wa_ref[h], preferred_element_type=F32))
        ias.append(jnp.dot(xh, wx_ref[h], preferred_element_type=F32))
    r = _sigmoid(jnp.concatenate(ras, axis=1) + ba_ref[...])
    ig = _sigmoid(jnp.concatenate(ias, axis=1) + bx_ref[...])
    nl = -lam_ref[...]
    softplus = jnp.maximum(nl, 0.0) + jnp.log1p(jnp.exp(-jnp.abs(nl)))
    log_a = (-RG_C * r) * softplus
    a_ref[...] = jnp.exp(log_a)
    th = jnp.tanh(log_a)
    u_ref[...] = jnp.sqrt((-2.0 * th) / (1.0 - th)) * (ig * xc)

    def step(i, h):
        off = pl.multiple_of(i * 8, 8)
        a8 = a_ref[pl.ds(off, 8), :]
        u8 = u_ref[pl.ds(off, 8), :]
        rows = []
        for k in range(8):
            h = a8[k:k + 1, :] * h + u8[k:k + 1, :]
            rows.append(h)
        hs_ref[pl.ds(off, 8), :] = jnp.concatenate(rows, axis=0)
        return h

    h_fin = lax.fori_loop(0, tc // 8, step, hc_ref[...])
    hc_ref[...] = h_fin
    hl_ref[0] = h_fin
    y_ref[...] = (_gelu(rg_ref[...]) * hs_ref[...]).astype(BF16)


def _rglru(z, cinit, h0, cw, cb, wa, ba, wx, bx, lam, nb, t, tc):
    nch = t // tc
    tb = tc // 8
    kern = functools.partial(_rglru_kernel, tc=tc)
    vec = lambda: pl.BlockSpec((1, D_RNN), lambda b, c: (0, 0))
    return pl.pallas_call(
        kern,
        grid=(nb, nch),
        in_specs=[pl.BlockSpec((tc, D_RNN), lambda b, c: (b * nch + c, 0)),
                  pl.BlockSpec((tc, D_RNN), lambda b, c: (b * nch + c, 1)),
                  pl.BlockSpec((8, D_RNN), lambda b, c: (jnp.maximum((b * nch + c) * tb - 1, 0), 0)),
                  pl.BlockSpec((1, 8, D_RNN), lambda b, c: (b, 0, 0)),
                  pl.BlockSpec((1, 1, D_RNN), lambda b, c: (b, 0, 0)),
                  pl.BlockSpec((4, D_RNN), lambda b, c: (0, 0)),
                  vec(),
                  pl.BlockSpec((RG_HEADS, RG_HEAD_DIM, RG_HEAD_DIM), lambda b, c: (0, 0, 0)),
                  vec(),
                  pl.BlockSpec((RG_HEADS, RG_HEAD_DIM, RG_HEAD_DIM), lambda b, c: (0, 0, 0)),
                  vec(),
                  vec()],
        out_specs=[pl.BlockSpec((tc, D_RNN), lambda b, c: (b * nch + c, 0)),
                   pl.BlockSpec((1, 1, D_RNN), lambda b, c: (b, 0, 0))],
        out_shape=[jax.ShapeDtypeStruct((nb * t, D_RNN), BF16),
                   jax.ShapeDtypeStruct((nb, 1, D_RNN), F32)],
        scratch_shapes=[pltpu.VMEM((tc + 8, D_RNN), F32), pltpu.VMEM((tc, D_RNN), F32),
                        pltpu.VMEM((tc, D_RNN), F32), pltpu.VMEM((tc, D_RNN), F32),
                        pltpu.VMEM((1, D_RNN), F32)],
        compiler_params=_cparams(2),
        name="rglru",
    )(z, z, z, cinit, h0, cw, cb, wa, ba, wx, bx, lam)


def _compress_tail(acc4, w2_ref, gain_ref, s_ref, o_ref, norm):
    hid = _gelu(acc4).astype(BF16)
    o = jnp.dot(hid, w2_ref[...], preferred_element_type=F32)
    if norm:
        o = _head_norm(o, gain_ref[...], s_ref[...])
    o_ref[...] = o.reshape(o_ref.shape)


def _compress_dense_kernel(x_ref, pe_ref, w1_ref, w2_ref, gain_ref, s_ref, o_ref, acc_ref, *, norm):
    k = pl.program_id(0)

    @pl.when(k == 0)
    def _():
        acc_ref[...] = jnp.zeros_like(acc_ref)

    xb = (x_ref[...] + pe_ref[...]).astype(BF16)
    acc_ref[...] += jnp.dot(xb, w1_ref[...], preferred_element_type=F32)

    @pl.when(k == pl.num_programs(0) - 1)
    def _():
        _compress_tail(acc_ref[...], w2_ref, gain_ref, s_ref, o_ref, norm)


def _compress_dense(x, pe, w1big, w2big, gain, s128, norm):
    r, kk = x.shape
    tk = 2048
    return pl.pallas_call(
        functools.partial(_compress_dense_kernel, norm=norm),
        grid=(kk // tk,),
        in_specs=[pl.BlockSpec((r, tk), lambda k: (0, k)),
                  pl.BlockSpec((1, tk), lambda k: (0, k)),
                  pl.BlockSpec((tk, D_KV), lambda k: (k, 0)),
                  pl.BlockSpec((D_KV, D_KV), lambda k: (0, 0)),
                  pl.BlockSpec((1, D_KV), lambda k: (0, 0)),
                  pl.BlockSpec((LANES, LANES), lambda k: (0, 0))],
        out_specs=pl.BlockSpec((r, D_KV), lambda k: (0, 0)),
        out_shape=jax.ShapeDtypeStruct((r, D_KV), F32),
        scratch_shapes=[pltpu.VMEM((r, D_KV), F32)],
        compiler_params=_cparams(1),
        name="compress_dense",
    )(x, pe, w1big, w2big, gain, s128)


def _compress_paged_kernel(pt_ref, pool_ref, pe_ref, w1_ref, w2_ref, gain_ref, s_ref, o_ref, buf, sem, *, norm):
    b = pl.program_id(0)
    cps = [pltpu.make_async_copy(pool_ref.at[pt_ref[b, p]], buf.at[pl.ds(p * D_KV, D_KV), :], sem.at[0])
           for p in range(N_PAGES)]
    for cp in cps:
        cp.start()
    for cp in cps:
        cp.wait()
    rows = N_PAGES * N_KV

    def body(d, acc):
        x = buf[pl.ds(d, rows, stride=HEAD_DIM), :] + jnp.tile(pe_ref[d], (rows // 8, 1))
        return acc + jnp.dot(x.astype(BF16), w1_ref[d], preferred_element_type=F32)

    acc = lax.fori_loop(0, HEAD_DIM, body, jnp.zeros((rows, LANES), F32))
    hid = _gelu(acc).astype(BF16)
    o = jnp.dot(hid, w2_ref[...], preferred_element_type=F32)
    if norm:
        ss = _group_sumsq(o, s_ref[...])
        o = o * lax.rsqrt(ss * (1.0 / HEAD_DIM) + EPS) * gain_ref[...]
    o_ref[0] = o


def _compress_paged(page_table, pool_t, pe_t, w1t, w2pair, gain2, s128, norm):
    rows = N_PAGES * N_KV
    grid_spec = pltpu.PrefetchScalarGridSpec(
        num_scalar_prefetch=1,
        grid=(DEC_BATCH,),
        in_specs=[pl.BlockSpec(memory_space=pl.ANY),
                  pl.BlockSpec((HEAD_DIM, 8, LANES), lambda b, pt: (0, 0, 0)),
                  pl.BlockSpec((HEAD_DIM, LANES, LANES), lambda b, pt: (0, 0, 0)),
                  pl.BlockSpec((LANES, LANES), lambda b, pt: (0, 0)),
                  pl.BlockSpec((1, LANES), lambda b, pt: (0, 0)),
                  pl.BlockSpec((LANES, LANES), lambda b, pt: (0, 0))],
        out_specs=pl.BlockSpec((1, rows, LANES), lambda b, pt: (b, 0, 0)),
        scratch_shapes=[pltpu.VMEM((N_PAGES * D_KV, LANES), F32), pltpu.SemaphoreType.DMA((1,))],
    )
    return pl.pallas_call(
        functools.partial(_compress_paged_kernel, norm=norm),
        grid_spec=grid_spec,
        out_shape=jax.ShapeDtypeStruct((DEC_BATCH, rows, LANES), F32),
        compiler_params=_cparams(1),
        name="compress_paged",
    )(page_table, pool_t, pe_t, w1t, w2pair, gain2, s128)


def _softmax_rows(s):
    m = jnp.max(s, axis=1, keepdims=True)
    p = jnp.exp(s - m)
    return p, jnp.sum(p, axis=1, keepdims=True)


def _masked_softmax_cols(s, valid):
    mx = jnp.max(s, axis=0, keepdims=True)
    e = jnp.where(valid, jnp.exp(s - mx), 0.0)
    den = jnp.sum(e, axis=0, keepdims=True)
    return e / jnp.maximum(den, 1e-30)


def _attn_prompt_kernel(q_ref, kc_ref, vc_ref, ks_ref, vs_ref, kw_ref, vw_ref, gt_ref,
                        bc_ref, bw_ref, bsn_ref, bsf_ref, e_ref, o_ref, am_ref):
    qb = pl.program_id(2)
    t0 = pl.multiple_of(qb * QT, QT)
    rows = HPG * QT
    q = q_ref[0, 0].reshape(rows, HEAD_DIM)

    bc = bc_ref[0, 0]
    st = lax.dot_general(kc_ref[0, 0], q, _NT, preferred_element_type=F32) + bc
    pt = _masked_softmax_cols(st, bc > 0.5 * NEG)
    o_c = lax.dot_general(pt.astype(BF16), vc_ref[0, 0], _TN, preferred_element_type=F32)

    imp = pt[:, 0:QT] + pt[:, QT:2 * QT] + pt[:, 2 * QT:3 * QT] + pt[:, 3 * QT:4 * QT]
    blk = lax.broadcasted_iota(jnp.int32, (NB_P, QT), 0)
    cur = (t0 + lax.broadcasted_iota(jnp.int32, (NB_P, QT), 1)) // BLOCK
    forced = (blk == cur) | (blk == cur - 1) | (blk == 0)
    imp = imp + jnp.where(forced, FORCE, 0.0)
    imp = jnp.where(blk > cur, NEG, imp)
    rank = jnp.zeros((NB_P, QT), F32)
    for m in range(NB_P):
        row = imp[m:m + 1, :]
        beats = (row > imp) | ((row == imp) & (blk > m))
        rank = rank + jnp.where(beats, 1.0, 0.0)
    sel = jnp.where(rank < float(TOPK), 1.0, 0.0).astype(BF16)
    mexp = lax.dot_general(sel, e_ref[...], _TN, preferred_element_type=F32)
    am_ref[...] = (mexp - 1.0) * (-NEG)

    def add_rows(s, am):
        n = s.shape[1]
        return (s.reshape(HPG, QT, n) + am[None]).reshape(rows, n)

    kn = ks_ref[0, 0, pl.ds(t0, 2 * QT), :]
    vn = vs_ref[0, 0, pl.ds(t0, 2 * QT), :]
    s = lax.dot_general(q, kn, _NT, preferred_element_type=F32) + bsn_ref[0]
    s = add_rows(s, am_ref[:, pl.ds(t0, 2 * QT)])
    m0 = jnp.max(s, axis=1, keepdims=True)
    p = jnp.exp(s - m0)
    l0 = jnp.sum(p, axis=1, keepdims=True)
    acc0 = jnp.dot(p.astype(BF16), vn, preferred_element_type=F32)

    def far(kt, carry):
        m, l, acc = carry
        off = pl.multiple_of(QT + kt * QT, QT)
        kf = ks_ref[0, 0, pl.ds(off, QT), :]
        vf = vs_ref[0, 0, pl.ds(off, QT), :]
        sf = lax.dot_general(q, kf, _NT, preferred_element_type=F32) + bsf_ref[0]
        sf = add_rows(sf, am_ref[:, pl.ds(off, QT)])
        mn = jnp.maximum(m, jnp.max(sf, axis=1, keepdims=True))
        alpha = jnp.exp(m - mn)
        pf = jnp.exp(sf - mn)
        l = alpha * l + jnp.sum(pf, axis=1, keepdims=True)
        acc = alpha * acc + jnp.dot(pf.astype(BF16), vf, preferred_element_type=F32)
        return mn, l, acc

    _, l_s, acc_s = lax.fori_loop(0, qb - 1, far, (m0, l0, acc0))
    o_s = acc_s / l_s

    span = WINDOW + QT
    kwin = kw_ref[0, 0, pl.ds(t0, span), :]
    vwin = vw_ref[0, 0, pl.ds(t0, span), :]
    sw = lax.dot_general(q, kwin, _NT, preferred_element_type=F32) + bw_ref[0]
    col = lax.broadcasted_iota(jnp.int32, (rows, span), 1)
    sw = jnp.where(col >= WINDOW - t0, sw, NEG)
    pw, lw = _softmax_rows(sw)
    o_w = jnp.dot(pw.astype(BF16), vwin, preferred_element_type=F32) / lw

    gt = gt_ref[0, 0]
    for h in range(HPG):
        sl = slice(h * QT, (h + 1) * QT)
        o = gt[:, h:h + 1] * o_c[sl] + gt[:, HPG + h:HPG + h + 1] * o_s[sl]
        o_ref[0, 0, h] = o + gt[:, 2 * HPG + h:2 * HPG + h + 1] * o_w[sl]


def _attn_prompt(q, kc, vc, ks, vs, kw, vw, gt, bc, bw, bsn, bsf, e):
    rows = HPG * QT
    kv = lambda n: pl.BlockSpec((1, 1, n, HEAD_DIM), lambda b, g, i: (b, g, 0, 0))
    return pl.pallas_call(
        _attn_prompt_kernel,
        grid=(BATCH, N_KV, N_QT),
        in_specs=[pl.BlockSpec((1, 1, HPG, QT, HEAD_DIM), lambda b, g, i: (b, g, 0, i, 0)),
                  kv(NB_P), kv(NB_P),
                  kv(QT + SEQ), kv(QT + SEQ),
                  kv(WINDOW + SEQ), kv(WINDOW + SEQ),
                  pl.BlockSpec((1, 1, QT, 3 * HPG), lambda b, g, i: (b, g, i, 0)),
                  pl.BlockSpec((1, 1, NB_P, rows), lambda b, g, i: (g, i, 0, 0)),
                  pl.BlockSpec((1, rows, WINDOW + QT), lambda b, g, i: (g, 0, 0)),
                  pl.BlockSpec((1, rows, 2 * QT), lambda b, g, i: (g, 0, 0)),
                  pl.BlockSpec((1, rows, QT), lambda b, g, i: (g, 0, 0)),
                  pl.BlockSpec((NB_P, QT + SEQ), lambda b, g, i: (0, 0))],
        out_specs=pl.BlockSpec((1, 1, HPG, QT, HEAD_DIM), lambda b, g, i: (b, g, 0, i, 0)),
        out_shape=jax.ShapeDtypeStruct((BATCH, N_KV, HPG, SEQ, HEAD_DIM), F32),
        scratch_shapes=[pltpu.VMEM((QT, QT + SEQ), F32)],
        compiler_params=_cparams(3),
        name="attn_prompt",
    )(q, kc, vc, ks, vs, kw, vw, gt, bc, bw, bsn, bsf, e)


S_ROWS = N_KV * DEC_SEQ * HPG
S_CHUNKS = 4
S_CHUNK_PAGES = N_PAGES // S_CHUNKS
S_CHUNK = S_CHUNK_PAGES * PAGE_SIZE


def _group_diag(x):
    rg = S_ROWS // N_KV
    return jnp.concatenate([x[g * rg:(g + 1) * rg, g * HEAD_DIM:(g + 1) * HEAD_DIM] for g in range(N_KV)], axis=0)


def _attn_sample_kernel(pt_ref, q_ref, kc_ref, vc_ref, bcs_ref, sk_ref, sv_ref, kn_ref, vn_ref,
                        cf_ref, bsn_ref, e_ref, wk_ref, wv_ref, kwn_ref, vwn_ref, bws_ref, g_ref,
                        o_ref, kbuf, vbuf, s_ref, imp_ref, semk, semv):
    b = pl.program_id(0)
    kcps, vcps = [], []
    for p in range(N_PAGES):
        page = pt_ref[b, p]
        keys = pl.ds(p * PAGE_SIZE, PAGE_SIZE)
        c = p // S_CHUNK_PAGES
        kcps.append(pltpu.make_async_copy(sk_ref.at[page], kbuf.at[:, keys], semk.at[c]))
        vcps.append(pltpu.make_async_copy(sv_ref.at[page], vbuf.at[:, keys], semv.at[c]))
    for cp in kcps:
        cp.start()
    for cp in vcps:
        cp.start()

    q = q_ref[0]

    bcs = bcs_ref[...]
    st = lax.dot_general(kc_ref[0], q, _NT, preferred_element_type=F32) + bcs
    pt = _masked_softmax_cols(st, bcs > 0.5 * NEG)
    o_c = _group_diag(lax.dot_general(pt.astype(BF16), vc_ref[0], _TN, preferred_element_type=F32))

    lane = lax.broadcasted_iota(jnp.int32, (NB_S_PAD, S_ROWS), 1)
    s1 = pt + jnp.where((lane & 1) == 0, pltpu.roll(pt, S_ROWS - 1, axis=1), pltpu.roll(pt, 1, axis=1))
    imp = s1 + jnp.where((lane & 2) == 0, pltpu.roll(s1, S_ROWS - 2, axis=1), pltpu.roll(s1, 2, axis=1))
    blk = lax.broadcasted_iota(jnp.int32, (NB_S_PAD, S_ROWS), 0)
    cur = PAST_LEN // BLOCK
    forced = (blk == cur) | (blk == cur - 1) | (blk == 0)
    imp = imp + jnp.where(forced, FORCE, 0.0)
    imp = jnp.where(blk > cur, NEG, imp)
    imp_ref[...] = imp

    def rank_step(m, rank):
        row = imp_ref[pl.ds(m, 1), :]
        beats = (row > imp) | ((row == imp) & (blk > m))
        return rank + jnp.where(beats, 1.0, 0.0)

    rank = lax.fori_loop(0, NB_S, rank_step, jnp.zeros((NB_S_PAD, S_ROWS), F32))
    sel = jnp.where(rank < float(TOPK), 1.0, 0.0).astype(BF16)[0:PAST_LEN // BLOCK]

    cf = cf_ref[...]
    near = bsn_ref[...]
    mx = None
    for c in range(S_CHUNKS):
        for cp in kcps[c * S_CHUNK_PAGES:(c + 1) * S_CHUNK_PAGES]:
            cp.wait()
        cols = slice(c * S_CHUNK, (c + 1) * S_CHUNK)
        s = jnp.dot(q, kbuf[:, cols].astype(BF16), preferred_element_type=F32)
        am = (lax.dot_general(sel, e_ref[:, cols], _TN, preferred_element_type=F32) - 1.0) * (-NEG)
        tiles = [cf] * (S_CHUNK // LANES)
        if c == S_CHUNKS - 1:
            tiles[-1] = near[:, 0:LANES]
        s = s + jnp.concatenate(tiles, axis=1) + am
        s_ref[:, cols] = s
        cm = jnp.max(s, axis=1, keepdims=True)
        mx = cm if mx is None else jnp.maximum(mx, cm)
    s_new = lax.dot_general(q, kn_ref[0], _NT, preferred_element_type=F32) + near[:, LANES:2 * LANES]
    mx = jnp.maximum(mx, jnp.max(s_new, axis=1, keepdims=True))
    p_new = jnp.exp(s_new - mx)
    l = jnp.sum(p_new, axis=1, keepdims=True)
    acc = jnp.dot(p_new.astype(BF16), vn_ref[0], preferred_element_type=F32)
    for c in range(S_CHUNKS):
        for cp in vcps[c * S_CHUNK_PAGES:(c + 1) * S_CHUNK_PAGES]:
            cp.wait()
        cols = slice(c * S_CHUNK, (c + 1) * S_CHUNK)
        p = jnp.exp(s_ref[:, cols] - mx)
        l = l + jnp.sum(p, axis=1, keepdims=True)
        acc = acc + lax.dot_general(p.astype(BF16), vbuf[:, cols].astype(BF16), _NT, preferred_element_type=F32)
    o_s = _group_diag(acc / l)

    bws = bws_ref[...]
    sw = jnp.dot(q, wk_ref[0].astype(BF16), preferred_element_type=F32) + bws[:, 0:WINDOW]
    swn = lax.dot_general(q, kwn_ref[0], _NT, preferred_element_type=F32) + bws[:, WINDOW:WINDOW + LANES]
    mw = jnp.maximum(jnp.max(sw, axis=1, keepdims=True), jnp.max(swn, axis=1, keepdims=True))
    pw = jnp.exp(sw - mw)
    pwn = jnp.exp(swn - mw)
    lw = jnp.sum(pw, axis=1, keepdims=True) + jnp.sum(pwn, axis=1, keepdims=True)
    accw = lax.dot_general(pw.astype(BF16), wv_ref[0].astype(BF16), _NT, preferred_element_type=F32)
    accw = accw + jnp.dot(pwn.astype(BF16), vwn_ref[0], preferred_element_type=F32)
    o_w = _group_diag(accw / lw)

    o_ref[0] = g_ref[0, 0] * o_c + g_ref[0, 1] * o_s + g_ref[0, 2] * o_w


def _attn_sample(page_table, qblk, kc, vc, bcs, pool_k, pool_v, kn, vn, cf, bsn, e, wk, wv, kwn, vwn, bws, gts):
    per_b = lambda *shape: pl.BlockSpec((1,) + shape, lambda b, pt: (b,) + (0,) * len(shape))
    const = lambda *shape: pl.BlockSpec(shape, lambda b, pt: (0,) * len(shape))
    grid_spec = pltpu.PrefetchScalarGridSpec(
        num_scalar_prefetch=1,
        grid=(DEC_BATCH,),
        in_specs=[per_b(S_ROWS, D_KV), per_b(NB_S_PAD, D_KV), per_b(NB_S_PAD, D_KV), const(NB_S_PAD, S_ROWS),
                  pl.BlockSpec(memory_space=pl.ANY), pl.BlockSpec(memory_space=pl.ANY),
                  per_b(LANES, D_KV), per_b(LANES, D_KV),
                  const(S_ROWS, LANES), const(S_ROWS, 2 * LANES), const(PAST_LEN // BLOCK, PAST_LEN),
                  per_b(D_KV, WINDOW), per_b(D_KV, WINDOW), per_b(LANES, D_KV), per_b(LANES, D_KV),
                  const(S_ROWS, WINDOW + LANES), per_b(3, S_ROWS, HEAD_DIM)],
        out_specs=per_b(S_ROWS, HEAD_DIM),
        scratch_shapes=[pltpu.VMEM((D_KV, PAST_LEN), F32), pltpu.VMEM((D_KV, PAST_LEN), F32),
                        pltpu.VMEM((S_ROWS, PAST_LEN), F32), pltpu.VMEM((NB_S_PAD, S_ROWS), F32),
                        pltpu.SemaphoreType.DMA((S_CHUNKS,)), pltpu.SemaphoreType.DMA((S_CHUNKS,))],
    )
    return pl.pallas_call(
        _attn_sample_kernel,
        grid_spec=grid_spec,
        out_shape=jax.ShapeDtypeStruct((DEC_BATCH, S_ROWS, HEAD_DIM), F32),
        compiler_params=_cparams(1),
        name="attn_sample",
    )(page_table, qblk, kc, vc, bcs, pool_k, pool_v, kn, vn, cf, bsn, e, wk, wv, kwn, vwn, bws, gts)


def _sconv_kernel(bg_ref, cg_ref, u_ref, hcg_ref, hu_ref, init_ref, w_ref, y_ref, st_ref, ext_ref, *, tc):
    c = pl.program_id(1)

    @pl.when(c == 0)
    def _():
        ext_ref[0:8, :] = init_ref[0]

    @pl.when(c > 0)
    def _():
        ext_ref[0:8, :] = hcg_ref[...] * hu_ref[...]

    p = cg_ref[...] * u_ref[...]
    ext_ref[8:8 + tc, :] = p
    w = w_ref[...]
    y = ext_ref[pl.ds(6, tc), :] * w[0:1, :]
    y = y + ext_ref[pl.ds(7, tc), :] * w[1:2, :]
    y = y + ext_ref[pl.ds(8, tc), :] * w[2:3, :]
    y_ref[...] = (bg_ref[...] * y).astype(BF16)
    st_ref[0] = p[tc - 8:tc, :]


def _sconv(z, init, w, nb, t, tc):
    nch = t // tc
    tb = tc // 8
    row = lambda b, c: b * nch + c
    halo = lambda col: pl.BlockSpec((8, D_CONV), lambda b, c: (jnp.maximum(row(b, c) * tb - 1, 0), col))
    return pl.pallas_call(
        functools.partial(_sconv_kernel, tc=tc),
        grid=(nb, nch),
        in_specs=[pl.BlockSpec((tc, D_CONV), lambda b, c: (row(b, c), 0)),
                  pl.BlockSpec((tc, D_CONV), lambda b, c: (row(b, c), 1)),
                  pl.BlockSpec((tc, D_CONV), lambda b, c: (row(b, c), 2)),
                  halo(1), halo(2),
                  pl.BlockSpec((1, 8, D_CONV), lambda b, c: (b, 0, 0)),
                  pl.BlockSpec((3, D_CONV), lambda b, c: (0, 0))],
        out_specs=[pl.BlockSpec((tc, D_CONV), lambda b, c: (row(b, c), 0)),
                   pl.BlockSpec((1, 8, D_CONV), lambda b, c: (b, 0, 0))],
        out_shape=[jax.ShapeDtypeStruct((nb * t, D_CONV), BF16),
                   jax.ShapeDtypeStruct((nb, 8, D_CONV), F32)],
        scratch_shapes=[pltpu.VMEM((tc + 8, D_CONV), F32)],
        compiler_params=_cparams(2),
        name="sconv",
    )(z, z, z, z, z, init, w)


def _bucket_np(dist):
    n = np.maximum(dist, 0)
    max_exact = N_BUCKETS // 2
    nf = np.maximum(n, 1).astype(np.float32)
    scaled = np.log(nf / np.float32(max_exact)) / np.float32(math.log(MAX_DIST / max_exact)) * np.float32(N_BUCKETS - max_exact)
    large = np.minimum(max_exact + scaled.astype(np.int32), N_BUCKETS - 1)
    return np.where(n < max_exact, n, large).astype(np.int32)


def _bias_table(tab_t, dist, valid):
    idx = _bucket_np(dist)
    heads = np.arange(dist.shape[0]).reshape((-1,) + (1,) * (dist.ndim - 1))
    vals = tab_t[heads, idx]
    return jnp.where(valid, vals, NEG)


def _prompt_tables(rel_bias):
    tab_t = rel_bias.T.astype(F32)
    i = np.arange(QT)
    t = (np.arange(N_QT) * QT)[:, None, None] + i[None, None, :]
    d = t - (np.arange(NB_P) * BLOCK + BLOCK - 1)[None, :, None]
    d = np.broadcast_to(d[None], (N_HEADS,) + d.shape)
    bc = _bias_table(tab_t, d, d >= 0)
    bc = bc.reshape(N_KV, HPG, N_QT, NB_P, QT).transpose(0, 2, 3, 1, 4).reshape(N_KV, N_QT, NB_P, HPG * QT)
    d = (i[:, None] + WINDOW) - np.arange(WINDOW + QT)[None, :]
    d = np.broadcast_to(d[None], (N_HEADS,) + d.shape)
    bw = _bias_table(tab_t, d, (d >= 0) & (d <= WINDOW)).reshape(N_KV, HPG * QT, WINDOW + QT)
    d = (i[:, None] + QT) - np.arange(2 * QT)[None, :]
    d = np.broadcast_to(d[None], (N_HEADS,) + d.shape)
    bsn = _bias_table(tab_t, d, d >= 0).reshape(N_KV, HPG * QT, 2 * QT)
    far = tab_t[:, N_BUCKETS - 1]
    bsf = jnp.broadcast_to(far[:, None, None], (N_HEADS, QT, QT)).reshape(N_KV, HPG * QT, QT)
    key_blk = np.concatenate([np.full((QT,), -1), np.arange(SEQ) // BLOCK])
    e = (key_blk[None, :] == np.arange(NB_P)[:, None]).astype(np.float32)
    return bc, bw, bsn, bsf, jnp.asarray(e, BF16)


def _sample_tables(rel_bias):
    tab_t = rel_bias.T.astype(F32)
    g, t, h = np.meshgrid(np.arange(N_KV), np.arange(DEC_SEQ), np.arange(HPG), indexing="ij")
    head = (g * HPG + h).reshape(-1)
    qpos = (PAST_LEN + t).reshape(-1)

    def rows_table(dist, valid):
        vals = tab_t[head[:, None], _bucket_np(dist)]
        return jnp.where(valid, vals, NEG)

    blk_end = np.arange(NB_S_PAD) * BLOCK + BLOCK - 1
    d = qpos[:, None] - blk_end[None, :]
    valid = (d >= 0) & (np.arange(NB_S_PAD)[None, :] < NB_S)
    bcs = rows_table(d, valid).T
    cf = jnp.broadcast_to(tab_t[head, N_BUCKETS - 1][:, None], (S_ROWS, LANES))
    kpos = np.concatenate([np.arange(PAST_LEN - LANES, PAST_LEN), PAST_LEN + np.arange(LANES)])
    d = qpos[:, None] - kpos[None, :]
    valid = (d >= 0) & (kpos[None, :] < PAST_LEN + DEC_SEQ)
    bsn = rows_table(d, valid)
    kpos = np.concatenate([np.arange(PAST_LEN - WINDOW, PAST_LEN), PAST_LEN + np.arange(LANES)])
    d = qpos[:, None] - kpos[None, :]
    valid = (d >= 0) & (d <= WINDOW) & (kpos[None, :] < PAST_LEN + DEC_SEQ)
    bws = rows_table(d, valid)
    key_blk = np.arange(PAST_LEN) // BLOCK
    e = (key_blk[None, :] == np.arange(PAST_LEN // BLOCK)[:, None]).astype(np.float32)
    return bcs, cf, bsn, bws, jnp.asarray(e, BF16)


def _feature_major(c):
    return c.transpose(0, 2, 3, 1).reshape(c.shape[0], D_KV, c.shape[1])


def _block_diag(w, n):
    a, b = w.shape
    return (jnp.eye(n, dtype=w.dtype)[:, None, :, None] * w[None, :, None, :]).reshape(n * a, n * b)


def _compress_weights(pe, w1, w2):
    pe_rows = pe.reshape(BLOCK, D_KV)
    eye = jnp.eye(N_KV, dtype=F32)
    w1big = (w1[:, None, :, None, :] * eye[None, :, None, :, None]).reshape(BLOCK * D_KV, D_KV).astype(BF16)
    w2big = _block_diag(w2, N_KV).astype(BF16)
    eye2 = jnp.eye(2, dtype=F32)
    w1t = (w1.transpose(1, 0, 2)[:, None, :, None, :] * eye2[None, :, None, :, None]).reshape(HEAD_DIM, LANES, LANES).astype(BF16)
    pe_d = jnp.tile(pe.transpose(2, 1, 0), (1, 2, 2))
    w2pair = _block_diag(w2, 2).astype(BF16)
    return pe_rows, w1big, w2big, pe_d, w1t, w2pair


def _layer0(y_p, y_s, norm_g, w_in, w_out, rel_bias, page_table, cache_cmp_k, cache_cmp_v, cache_sel_k,
            cache_sel_v, cache_win_k, cache_win_v, state_h, state_conv, rg_conv_w, rg_conv_b, rg_wa, rg_ba,
            rg_wx, rg_bx, rg_lambda, q_norm, k_norm, pe_k, pe_v, w1_k, w2_k, w1_v, w2_v):
    w_in_b = jnp.pad(w_in, ((0, 0), (0, D_IN_E_PAD - D_IN_E))).astype(BF16)
    w_out_b = w_out.astype(BF16)
    z_p = _norm_matmul(y_p, norm_g, w_in_b, 512, D_IN_E_PAD // 2)
    z_s = _norm_matmul(y_s, norm_g, w_in_b, 256, D_IN_E_PAD // 2)

    half = np.arange(LANES) // HEAD_DIM
    s128 = jnp.asarray(half[:, None] == half[None, :], BF16)
    q_gain = jnp.tile(q_norm, N_HEADS).reshape(1, D_ATTN)
    kc_gain = jnp.tile(k_norm[0], N_KV).reshape(1, D_KV)
    ks_gain = jnp.tile(k_norm[1], N_KV).reshape(1, D_KV)
    kw_gain = jnp.tile(k_norm[2], N_KV).reshape(1, D_KV)
    qn_p, ksn_p, kwn_p, gt_p = _prep0(z_p, q_gain, ks_gain, kw_gain, s128, 512)
    qn_s, ksn_s, kwn_s, gt_s = _prep0(z_s, q_gain, ks_gain, kw_gain, s128, 256)

    cw = rg_conv_w
    cb = rg_conv_b.reshape(1, D_RNN)
    wa = rg_wa.astype(BF16)
    wx = rg_wx.astype(BF16)
    ba = rg_ba.reshape(1, D_RNN)
    bx = rg_bx.reshape(1, D_RNN)
    lam = rg_lambda.reshape(1, D_RNN)
    yr_p, hl_p = _rglru(z_p, jnp.zeros((BATCH, 8, D_RNN), F32), jnp.zeros((BATCH, 1, D_RNN), F32),
                        cw, cb, wa, ba, wx, bx, lam, BATCH, SEQ, 256)
    cinit_s = jnp.pad(state_conv, ((0, 0), (5, 0), (0, 0)))
    yr_s, hl_s = _rglru(z_s, cinit_s, state_h.reshape(DEC_BATCH, 1, D_RNN),
                        cw, cb, wa, ba, wx, bx, lam, DEC_BATCH, DEC_SEQ, DEC_SEQ)

    col = lambda z, k: z[:, 3072 + 256 * k:3328 + 256 * k]
    kc_p, vc_p, vs_p, vw_p = col(z_p, 0), col(z_p, 1), col(z_p, 3), col(z_p, 5)
    kc_s, vc_s, vs_s, vw_s = col(z_s, 0), col(z_s, 1), col(z_s, 3), col(z_s, 5)

    pe_k_rows, w1big_k, w2big_k, pe_k_d, w1t_k, w2pair_k = _compress_weights(pe_k, w1_k, w2_k)
    pe_v_rows, w1big_v, w2big_v, pe_v_d, w1t_v, w2pair_v = _compress_weights(pe_v, w1_v, w2_v)
    n_new = DEC_SEQ * D_KV

    def dense_rows(c_p, c_s):
        tail = jnp.pad(c_s.reshape(DEC_BATCH, n_new), ((0, 0), (0, BLOCK * D_KV - n_new)))
        return jnp.concatenate([c_p.reshape(BATCH * NB_P, BLOCK * D_KV), tail], axis=0)

    kcb_d = _compress_dense(dense_rows(kc_p, kc_s), pe_k_rows.reshape(1, -1), w1big_k, w2big_k, kc_gain, s128, True)
    vcb_d = _compress_dense(dense_rows(vc_p, vc_s), pe_v_rows.reshape(1, -1), w1big_v, w2big_v, kc_gain, s128, False)
    gain2 = jnp.tile(k_norm[0], 2).reshape(1, LANES)
    kcb_pg = _compress_paged(page_table, cache_cmp_k, pe_k_d, w1t_k, w2pair_k, gain2, s128, True)
    vcb_pg = _compress_paged(page_table, cache_cmp_v, pe_v_d, w1t_v, w2pair_v, gain2, s128, False)
    n_pb = BATCH * NB_P

    def sample_blocks(paged, dense):
        paged = paged.reshape(DEC_BATCH, N_PAGES, N_KV, 2, HEAD_DIM).transpose(0, 1, 3, 2, 4)
        paged = paged.reshape(DEC_BATCH, PAST_LEN // BLOCK, D_KV)
        x = jnp.concatenate([paged, dense[n_pb:].reshape(DEC_BATCH, 1, D_KV)], axis=1)
        return jnp.pad(x, ((0, 0), (0, NB_S_PAD - NB_S), (0, 0))).astype(BF16)

    kcb_s, vcb_s = sample_blocks(kcb_pg, kcb_d), sample_blocks(vcb_pg, vcb_d)

    def group_major(x, pad):
        x = x.astype(BF16).reshape(BATCH, -1, N_KV, HEAD_DIM).transpose(0, 2, 1, 3)
        return jnp.pad(x, ((0, 0), (0, 0), (pad, 0), (0, 0)))

    q_r = qn_p.reshape(BATCH, SEQ, N_KV, HPG, HEAD_DIM).transpose(0, 2, 3, 1, 4)
    gt_r = gt_p[:, :3 * N_HEADS].reshape(BATCH, SEQ, 3, N_KV, HPG).transpose(0, 3, 1, 2, 4).reshape(BATCH, N_KV, SEQ, 3 * HPG)
    bc, bw, bsn, bsf, e_p = _prompt_tables(rel_bias)
    o_r = _attn_prompt(q_r, group_major(kcb_d[:n_pb], 0), group_major(vcb_d[:n_pb], 0),
                       group_major(ksn_p, QT), group_major(vs_p, QT),
                       group_major(kwn_p, WINDOW), group_major(vw_p, WINDOW),
                       gt_r, bc, bw, bsn, bsf, e_p)
    o_p = o_r.transpose(0, 3, 1, 2, 4).reshape(BATCH * SEQ, D_ATTN).astype(BF16)

    q_g = qn_s.reshape(DEC_BATCH, DEC_SEQ, N_KV, HPG, HEAD_DIM).transpose(0, 2, 1, 3, 4)
    eye = jnp.eye(N_KV, dtype=BF16)
    qblk = (q_g[:, :, :, :, None, :] * eye[None, :, None, None, :, None]).reshape(DEC_BATCH, S_ROWS, D_KV)
    gts = gt_s[:, :3 * N_HEADS].reshape(DEC_BATCH, DEC_SEQ, 3, N_KV, HPG).transpose(0, 2, 3, 1, 4)
    gts = jnp.broadcast_to(gts.reshape(DEC_BATCH, 3, S_ROWS, 1), (DEC_BATCH, 3, S_ROWS, HEAD_DIM))
    bcs, cf, bsn_s, bws, e_s = _sample_tables(rel_bias)
    per_seq = lambda x: jnp.pad(x.astype(BF16).reshape(DEC_BATCH, DEC_SEQ, D_KV), ((0, 0), (0, LANES - DEC_SEQ), (0, 0)))
    o_sr = _attn_sample(page_table, qblk, kcb_s, vcb_s, bcs, cache_sel_k, cache_sel_v,
                        per_seq(ksn_s), per_seq(vs_s), cf, bsn_s, e_s,
                        _feature_major(cache_win_k), _feature_major(cache_win_v),
                        per_seq(kwn_s), per_seq(vw_s), bws, gts)
    o_s = o_sr.reshape(DEC_BATCH, N_KV, DEC_SEQ, HPG, HEAD_DIM).transpose(0, 2, 1, 3, 4).reshape(DEC_BATCH * DEC_SEQ, D_ATTN).astype(BF16)

    y_p = _proj_res(y_p, yr_p, 0, o_p, 0, w_out_b, 512)
    y_s = _proj_res(y_s, yr_s, 0, o_s, 0, w_out_b, 256)

    def kv_out(x, nb, t):
        return x.reshape(1, nb, t, N_KV, HEAD_DIM)

    win_k_s = jnp.concatenate([cache_win_k, kwn_s.reshape(DEC_BATCH, DEC_SEQ, N_KV, HEAD_DIM)], axis=1)[None, :, DEC_SEQ:]
    win_v_s = jnp.concatenate([cache_win_v, vw_s.reshape(DEC_BATCH, DEC_SEQ, N_KV, HEAD_DIM)], axis=1)[None, :, DEC_SEQ:]
    state = dict(
        cmp_k_p=kv_out(kc_p, BATCH, SEQ), cmp_k_s=kv_out(kc_s, DEC_BATCH, DEC_SEQ),
        cmp_v_p=kv_out(vc_p, BATCH, SEQ), cmp_v_s=kv_out(vc_s, DEC_BATCH, DEC_SEQ),
        sel_k_p=kv_out(ksn_p, BATCH, SEQ), sel_k_s=kv_out(ksn_s, DEC_BATCH, DEC_SEQ),
        sel_v_p=kv_out(vs_p, BATCH, SEQ), sel_v_s=kv_out(vs_s, DEC_BATCH, DEC_SEQ),
        win_k_p=kv_out(kwn_p, BATCH, SEQ)[:, :, SEQ - WINDOW:], win_k_s=win_k_s,
        win_v_p=kv_out(vw_p, BATCH, SEQ)[:, :, SEQ - WINDOW:], win_v_s=win_v_s,
        rglru_h_p=hl_p.reshape(1, BATCH, D_RNN), rglru_h_s=hl_s.reshape(1, DEC_BATCH, D_RNN),
        rglru_conv_p=z_p[:, :D_RNN].reshape(1, BATCH, SEQ, D_RNN)[:, :, SEQ - 3:],
        rglru_conv_s=z_s[:, :D_RNN].reshape(1, DEC_BATCH, DEC_SEQ, D_RNN)[:, :, DEC_SEQ - 3:],
    )
    return y_p, y_s, state


def _layer1(y_p, y_s, norm_g, w_in, sc_w, w_out, state_sconv):
    w_in_b = w_in.astype(BF16)
    w_out_b = w_out.astype(BF16)
    z_p = _norm_matmul(y_p, norm_g, w_in_b, 512, 2048)
    z_s = _norm_matmul(y_s, norm_g, w_in_b, 256, 2048)
    yb_p, st_p = _sconv(z_p, jnp.zeros((BATCH, 8, D_CONV), F32), sc_w, BATCH, SEQ, 256)
    yb_s, st_s = _sconv(z_s, jnp.pad(state_sconv, ((0, 0), (6, 0), (0, 0))), sc_w, DEC_BATCH, DEC_SEQ, DEC_SEQ)
    y_p = _proj_res(y_p, yb_p, 0, yb_p, 1, w_out_b, 512)
    y_s = _proj_res(y_s, yb_s, 0, yb_s, 1, w_out_b, 256)
    return y_p, y_s, st_p[None, :, 6:8], st_s[None, :, 6:8]


def kernel(x_prompt, x_sample, cache_cmp_k, cache_cmp_v, cache_sel_k, cache_sel_v, cache_win_k, cache_win_v, state_rglru_h, state_rglru_conv, state_sconv, page_table, rel_bias, norm_mix, norm_ffn, w_ff1, w_ff2, w_in_e, w_out_e, rg_conv_w, rg_conv_b, rg_wa, rg_ba, rg_wx, rg_bx, rg_lambda, q_norm, k_norm, cmp_pe_k, cmp_pe_v, cmp_w1_k, cmp_w2_k, cmp_w1_v, cmp_w2_v, w_in_o, sc_w, w_out_o):
    y_p = x_prompt.reshape(BATCH * SEQ, D_MODEL)
    y_s = x_sample.reshape(DEC_BATCH * DEC_SEQ, D_MODEL)
    pool = lambda c: _feature_major(c[0])

    y_p, y_s, st = _layer0(
        y_p, y_s, norm_mix[0], w_in_e[0], w_out_e[0], rel_bias, page_table,
        pool(cache_cmp_k), pool(cache_cmp_v), pool(cache_sel_k), pool(cache_sel_v), cache_win_k[0], cache_win_v[0],
        state_rglru_h[0], state_rglru_conv[0], rg_conv_w[0], rg_conv_b[0], rg_wa[0], rg_ba[0], rg_wx[0],
        rg_bx[0], rg_lambda[0], q_norm[0], k_norm[0], cmp_pe_k[0], cmp_pe_v[0], cmp_w1_k[0], cmp_w2_k[0],
        cmp_w1_v[0], cmp_w2_v[0])
    w1 = w_ff1.astype(BF16)
    w2 = w_ff2.astype(BF16)
    y_p = _ffn(y_p, norm_ffn[0], w1[0], w2[0], 512, 512)
    y_s = _ffn(y_s, norm_ffn[0], w1[0], w2[0], 256, 512)

    y_p, y_s, sconv_p, sconv_s = _layer1(y_p, y_s, norm_mix[1], w_in_o[0], sc_w[0], w_out_o[0], state_sconv[0])
    y_p = _ffn(y_p, norm_ffn[1], w1[1], w2[1], 512, 512)
    y_s = _ffn(y_s, norm_ffn[1], w1[1], w2[1], 256, 512)

    return (y_p.reshape(BATCH, SEQ, D_MODEL), y_s.reshape(DEC_BATCH, DEC_SEQ, D_MODEL),
            st["cmp_k_p"], st["cmp_k_s"], st["cmp_v_p"], st["cmp_v_s"],
            st["sel_k_p"], st["sel_k_s"], st["sel_v_p"], st["sel_v_s"],
            st["win_k_p"], st["win_k_s"], st["win_v_p"], st["win_v_s"],
            st["rglru_h_p"], st["rglru_h_s"], st["rglru_conv_p"], st["rglru_conv_s"],
            sconv_p, sconv_s)
```

```python
import functools
import math

import numpy as np
import jax
import jax.numpy as jnp
from jax import lax
from jax.experimental import pallas as pl
from jax.experimental.pallas import tpu as pltpu

F32 = jnp.float32
BF16 = jnp.bfloat16

D_MODEL = 2048
BATCH = 4
SEQ = 2048
DEC_BATCH = 32
DEC_SEQ = 8
PAST_LEN = 8192
PAGE_SIZE = 128
D_RNN = 1024
RG_HEADS = 8
RG_HEAD_DIM = 128
RG_C = 8.0
N_HEADS = 16
HEAD_DIM = 64
N_KV = 4
HPG = 4
D_ATTN = 1024
D_KV = 256
BLOCK = 64
TOPK = 16
WINDOW = 512
SCALE = HEAD_DIM ** -0.5
FORCE = 1e4
NEG = -1e30
N_BUCKETS = 32
MAX_DIST = 128
D_CONV = 2048
D_FF = 8192
EPS = 1e-6
D_IN_E = 4656
D_IN_E_PAD = 4864

N_PAGES = PAST_LEN // PAGE_SIZE
NB_P = SEQ // BLOCK
NB_PAST = PAST_LEN // BLOCK
NB_S = NB_PAST + 1
NB_S_PAD = 136
QT = 128
N_QT = SEQ // QT
Q_ROWS = HPG * QT
FAR = 4 * QT

LANES = 128
VMEM_LIMIT = 52 * 1024 * 1024

_NT = (((1,), (1,)), ((), ()))
_TN = (((0,), (0,)), ((), ()))


def _cparams(n_axes, vmem=VMEM_LIMIT):
    return pltpu.CompilerParams(dimension_semantics=("arbitrary",) * n_axes, vmem_limit_bytes=vmem)


def _gelu(x):
    return 0.5 * x * (1.0 + jnp.tanh(0.7978845608028654 * (x + 0.044715 * (x * x * x))))


def _sigmoid(x):
    return 1.0 / (1.0 + jnp.exp(-x))


def _rms_rows(x, g):
    ms = jnp.mean(x * x, axis=-1, keepdims=True)
    return (x * lax.rsqrt(ms + EPS)) * g


def _norm_matmul_kernel(x_ref, g_ref, w_ref, o_ref, xn_ref):
    @pl.when(pl.program_id(1) == 0)
    def _():
        xn_ref[...] = _rms_rows(x_ref[...], g_ref[...]).astype(BF16)

    o_ref[...] = jnp.dot(xn_ref[...], w_ref[...], preferred_element_type=F32)


def _norm_matmul(x, g, w, tm, tn):
    m, d = x.shape
    n = w.shape[1]
    return pl.pallas_call(
        _norm_matmul_kernel,
        grid=(m // tm, n // tn),
        in_specs=[pl.BlockSpec((tm, d), lambda i, j: (i, 0)),
                  pl.BlockSpec((1, d), lambda i, j: (0, 0)),
                  pl.BlockSpec((d, tn), lambda i, j: (0, j))],
        out_specs=pl.BlockSpec((tm, tn), lambda i, j: (i, j)),
        out_shape=jax.ShapeDtypeStruct((m, n), F32),
        scratch_shapes=[pltpu.VMEM((tm, d), BF16)],
        compiler_params=_cparams(2),
        name="norm_matmul",
    )(x, g.reshape(1, d), w)


def _ffn_kernel(x_ref, g_ref, w1_ref, w2_ref, o_ref, xn_ref):
    j = pl.program_id(1)

    @pl.when(j == 0)
    def _():
        x = x_ref[...]
        xn_ref[...] = _rms_rows(x, g_ref[...]).astype(BF16)
        o_ref[...] = x

    h = jnp.dot(xn_ref[...], w1_ref[...], preferred_element_type=F32)
    a = jnp.maximum(h, 0.0)
    a = (a * a).astype(BF16)
    o_ref[...] += jnp.dot(a, w2_ref[...], preferred_element_type=F32)


def _ffn(x, g, w1, w2, tm, tf):
    m, d = x.shape
    f = w1.shape[1]
    return pl.pallas_call(
        _ffn_kernel,
        grid=(m // tm, f // tf),
        in_specs=[pl.BlockSpec((tm, d), lambda i, j: (i, 0)),
                  pl.BlockSpec((1, d), lambda i, j: (0, 0)),
                  pl.BlockSpec((d, tf), lambda i, j: (0, j)),
                  pl.BlockSpec((tf, d), lambda i, j: (j, 0))],
        out_specs=pl.BlockSpec((tm, d), lambda i, j: (i, 0)),
        out_shape=jax.ShapeDtypeStruct((m, d), F32),
        scratch_shapes=[pltpu.VMEM((tm, d), BF16)],
        compiler_params=_cparams(2),
        name="ffn",
    )(x, g.reshape(1, d), w1, w2)


def _proj_res_kernel(x_ref, a1_ref, a2_ref, w1_ref, w2_ref, o_ref):
    acc = jnp.dot(a1_ref[...], w1_ref[...], preferred_element_type=F32)
    acc = acc + jnp.dot(a2_ref[...], w2_ref[...], preferred_element_type=F32)
    o_ref[...] = x_ref[...] + acc


def _proj_res(x, a1, a1_col, a2, a2_col, w, tm):
    m, d = x.shape
    kh = w.shape[0] // 2
    return pl.pallas_call(
        _proj_res_kernel,
        grid=(m // tm,),
        in_specs=[pl.BlockSpec((tm, d), lambda i: (i, 0)),
                  pl.BlockSpec((tm, kh), lambda i: (i, a1_col)),
                  pl.BlockSpec((tm, kh), lambda i: (i, a2_col)),
                  pl.BlockSpec((kh, d), lambda i: (0, 0)),
                  pl.BlockSpec((kh, d), lambda i: (1, 0))],
        out_specs=pl.BlockSpec((tm, d), lambda i: (i, 0)),
        out_shape=jax.ShapeDtypeStruct((m, d), F32),
        compiler_params=_cparams(1),
        name="proj_res",
    )(x, a1, a2, w, w)


def _group_sumsq(x, s):
    x2 = x * x
    hi = x2.astype(BF16)
    lo = (x2 - hi.astype(F32)).astype(BF16)
    return jnp.dot(hi, s, preferred_element_type=F32) + jnp.dot(lo, s, preferred_element_type=F32)


def _head_norm(x, gain, s):
    outs = []
    for c in range(x.shape[1] // LANES):
        xc = x[:, c * LANES:(c + 1) * LANES]
        ss = _group_sumsq(xc, s)
        outs.append(xc * lax.rsqrt(ss * (1.0 / HEAD_DIM) + EPS))
    return jnp.concatenate(outs, axis=1) * gain


def _prep0_kernel(q_ref, ks_ref, kw_ref, gl_ref, qg_ref, ksg_ref, kwg_ref, s_ref,
                  qn_ref, ksn_ref, kwn_ref, gt_ref):
    s = s_ref[...]
    qn_ref[...] = (_head_norm(q_ref[...], qg_ref[...], s) * SCALE).astype(BF16)
    ksn_ref[...] = _head_norm(ks_ref[...], ksg_ref[...], s)
    kwn_ref[...] = _head_norm(kw_ref[...], kwg_ref[...], s)
    gt_ref[...] = _sigmoid(gl_ref[...])


def _prep0(z, q_gain, ks_gain, kw_gain, s128, tm):
    m = z.shape[0]
    return pl.pallas_call(
        _prep0_kernel,
        grid=(m // tm,),
        in_specs=[pl.BlockSpec((tm, D_ATTN), lambda i: (i, 2)),
                  pl.BlockSpec((tm, D_KV), lambda i: (i, 14)),
                  pl.BlockSpec((tm, D_KV), lambda i: (i, 16)),
                  pl.BlockSpec((tm, LANES), lambda i: (i, 36)),
                  pl.BlockSpec((1, D_ATTN), lambda i: (0, 0)),
                  pl.BlockSpec((1, D_KV), lambda i: (0, 0)),
                  pl.BlockSpec((1, D_KV), lambda i: (0, 0)),
                  pl.BlockSpec((LANES, LANES), lambda i: (0, 0))],
        out_specs=[pl.BlockSpec((tm, D_ATTN), lambda i: (i, 0)),
                   pl.BlockSpec((tm, D_KV), lambda i: (i, 0)),
                   pl.BlockSpec((tm, D_KV), lambda i: (i, 0)),
                   pl.BlockSpec((tm, LANES), lambda i: (i, 0))],
        out_shape=[jax.ShapeDtypeStruct((m, D_ATTN), BF16),
                   jax.ShapeDtypeStruct((m, D_KV), F32),
                   jax.ShapeDtypeStruct((m, D_KV), F32),
                   jax.ShapeDtypeStruct((m, LANES), F32)],
        compiler_params=_cparams(1),
        name="prep0",
    )(z, z, z, z, q_gain, ks_gain, kw_gain, s128)


def _rglru_kernel(rx_ref, rg_ref, halo_ref, cinit_ref, h0_ref, cw_ref, cb_ref, wa_ref, ba_ref,
                  wx_ref, bx_ref, lam_ref, y_ref, hl_ref, ext_ref, a_ref, u_ref, hs_ref, hc_ref, *, tc):
    c = pl.program_id(1)

    @pl.when(c == 0)
    def _():
        ext_ref[0:8, :] = cinit_ref[0]
        hc_ref[...] = h0_ref[0]

    @pl.when(c > 0)
    def _():
        ext_ref[0:8, :] = halo_ref[...]

    ext_ref[8:8 + tc, :] = rx_ref[...]
    w = cw_ref[...]
    xc = ext_ref[pl.ds(5, tc), :] * w[0:1, :]
    xc = xc + ext_ref[pl.ds(6, tc), :] * w[1:2, :]
    xc = xc + ext_ref[pl.ds(7, tc), :] * w[2:3, :]
    xc = xc + ext_ref[pl.ds(8, tc), :] * w[3:4, :]
    xc = xc + cb_ref[...]

    ras, ias = [], []
    for h in range(RG_HEADS):
        xh = xc[:, h * RG_HEAD_DIM:(h + 1) * RG_HEAD_DIM].astype(BF16)
        ras.append(jnp.dot(xh, wa_ref[h], preferred_element_type=F32))
        ias.append(jnp.dot(xh, wx_ref[h], preferred_element_type=F32))
    r = _sigmoid(jnp.concatenate(ras, axis=1) + ba_ref[...])
    ig = _sigmoid(jnp.concatenate(ias, axis=1) + bx_ref[...])
    nl = -lam_ref[...]
    softplus = jnp.maximum(nl, 0.0) + jnp.log1p(jnp.exp(-jnp.abs(nl)))
    log_a = (-RG_C * r) * softplus
    a_ref[...] = jnp.exp(log_a)
    th = jnp.tanh(log_a)
    u_ref[...] = jnp.sqrt((-2.0 * th) / (1.0 - th)) * (ig * xc)

    def step(i, h):
        off = pl.multiple_of(i * 8, 8)
        a8 = a_ref[pl.ds(off, 8), :]
        u8 = u_ref[pl.ds(off, 8), :]
        rows = []
        for k in range(8):
            h = a8[k:k + 1, :] * h + u8[k:k + 1, :]
            rows.append(h)
        hs_ref[pl.ds(off, 8), :] = jnp.concatenate(rows, axis=0)
        return h

    h_fin = lax.fori_loop(0, tc // 8, step, hc_ref[...])
    hc_ref[...] = h_fin
    hl_ref[0] = h_fin
    y_ref[...] = (_gelu(rg_ref[...]) * hs_ref[...]).astype(BF16)


def _rglru(z, cinit, h0, cw, cb, wa, ba, wx, bx, lam, nb, t, tc):
    nch = t // tc
    tb = tc // 8
    kern = functools.partial(_rglru_kernel, tc=tc)
    vec = lambda: pl.BlockSpec((1, D_RNN), lambda b, c: (0, 0))
    return pl.pallas_call(
        kern,
        grid=(nb, nch),
        in_specs=[pl.BlockSpec((tc, D_RNN), lambda b, c: (b * nch + c, 0)),
                  pl.BlockSpec((tc, D_RNN), lambda b, c: (b * nch + c, 1)),
                  pl.BlockSpec((8, D_RNN), lambda b, c: (jnp.maximum((b * nch + c) * tb - 1, 0), 0)),
                  pl.BlockSpec((1, 8, D_RNN), lambda b, c: (b, 0, 0)),
                  pl.BlockSpec((1, 1, D_RNN), lambda b, c: (b, 0, 0)),
                  pl.BlockSpec((4, D_RNN), lambda b, c: (0, 0)),
                  vec(),
                  pl.BlockSpec((RG_HEADS, RG_HEAD_DIM, RG_HEAD_DIM), lambda b, c: (0, 0, 0)),
                  vec(),
                  pl.BlockSpec((RG_HEADS, RG_HEAD_DIM, RG_HEAD_DIM), lambda b, c: (0, 0, 0)),
                  vec(),
                  vec()],
        out_specs=[pl.BlockSpec((tc, D_RNN), lambda b, c: (b * nch + c, 0)),
                   pl.BlockSpec((1, 1, D_RNN), lambda b, c: (b, 0, 0))],
        out_shape=[jax.ShapeDtypeStruct((nb * t, D_RNN), BF16),
                   jax.ShapeDtypeStruct((nb, 1, D_RNN), F32)],
        scratch_shapes=[pltpu.VMEM((tc + 8, D_RNN), F32), pltpu.VMEM((tc, D_RNN), F32),
                        pltpu.VMEM((tc, D_RNN), F32), pltpu.VMEM((tc, D_RNN), F32),
                        pltpu.VMEM((1, D_RNN), F32)],
        compiler_params=_cparams(2),
        name="rglru",
    )(z, z, z, cinit, h0, cw, cb, wa, ba, wx, bx, lam)


def _compress_dense_kernel(x_ref, pe_ref, w1_ref, w2_ref, gain_ref, s_ref, o_ref, acc_ref, *, norm):
    k = pl.program_id(0)

    @pl.when(k == 0)
    def _():
        acc_ref[...] = jnp.zeros_like(acc_ref)

    xb = (x_ref[...] + pe_ref[...]).astype(BF16)
    acc_ref[...] += jnp.dot(xb, w1_ref[...], preferred_element_type=F32)

    @pl.when(k == pl.num_programs(0) - 1)
    def _():
        hid = _gelu(acc_ref[...]).astype(BF16)
        o = jnp.dot(hid, w2_ref[...], preferred_element_type=F32)
        if norm:
            o = _head_norm(o, gain_ref[...], s_ref[...])
        o_ref[...] = o


def _compress_dense(x, pe, w1big, w2big, gain, s128, norm):
    r, kk = x.shape
    tk = 2048
    return pl.pallas_call(
        functools.partial(_compress_dense_kernel, norm=norm),
        grid=(kk // tk,),
        in_specs=[pl.BlockSpec((r, tk), lambda k: (0, k)),
                  pl.BlockSpec((1, tk), lambda k: (0, k)),
                  pl.BlockSpec((tk, D_KV), lambda k: (k, 0)),
                  pl.BlockSpec((D_KV, D_KV), lambda k: (0, 0)),
                  pl.BlockSpec((1, D_KV), lambda k: (0, 0)),
                  pl.BlockSpec((LANES, LANES), lambda k: (0, 0))],
        out_specs=pl.BlockSpec((r, D_KV), lambda k: (0, 0)),
        out_shape=jax.ShapeDtypeStruct((r, D_KV), F32),
        scratch_shapes=[pltpu.VMEM((r, D_KV), F32)],
        compiler_params=_cparams(1),
        name="compress_dense",
    )(x, pe, w1big, w2big, gain, s128)


PG_ROWS = N_PAGES * N_KV


def _page_scatter_copies(pt_ref, seq, pool_ref, buf, slot, sem):
    cps = []
    for p in range(N_PAGES):
        page = pt_ref[seq, p]
        for g in range(N_KV):
            cps.append(pltpu.make_async_copy(pool_ref.at[page, pl.ds(g * HEAD_DIM, HEAD_DIM), :],
                                             buf.at[slot, :, p * N_KV + g, :], sem.at[slot]))
    return cps


def _compress_paged_kernel(pt_ref, pool_ref, pe_ref, w1_ref, w2_ref, gain_ref, s_ref, o_ref, buf, sem, *, norm):
    b = pl.program_id(0)
    slot = b % 2

    @pl.when(b == 0)
    def _():
        for cp in _page_scatter_copies(pt_ref, 0, pool_ref, buf, 0, sem):
            cp.start()

    @pl.when(b + 1 < pl.num_programs(0))
    def _():
        for cp in _page_scatter_copies(pt_ref, b + 1, pool_ref, buf, 1 - slot, sem):
            cp.start()

    for cp in _page_scatter_copies(pt_ref, b, pool_ref, buf, slot, sem):
        cp.wait()

    acc = jnp.zeros((PG_ROWS, LANES), F32)
    for d in range(HEAD_DIM):
        x = buf[slot, d] + jnp.tile(pe_ref[d], (PG_ROWS // 8, 1))
        acc = acc + jnp.dot(x.astype(BF16), w1_ref[d], preferred_element_type=F32)
    hid = _gelu(acc).astype(BF16)
    o = jnp.dot(hid, w2_ref[...], preferred_element_type=F32)
    if norm:
        ss = _group_sumsq(o, s_ref[...])
        o = o * lax.rsqrt(ss * (1.0 / HEAD_DIM) + EPS) * gain_ref[...]
    o_ref[0] = o


def _compress_paged(page_table, pool_t, pe_t, w1t, w2pair, gain2, s128, norm):
    grid_spec = pltpu.PrefetchScalarGridSpec(
        num_scalar_prefetch=1,
        grid=(DEC_BATCH,),
        in_specs=[pl.BlockSpec(memory_space=pl.ANY),
                  pl.BlockSpec((HEAD_DIM, 8, LANES), lambda b, pt: (0, 0, 0)),
                  pl.BlockSpec((HEAD_DIM, LANES, LANES), lambda b, pt: (0, 0, 0)),
                  pl.BlockSpec((LANES, LANES), lambda b, pt: (0, 0)),
                  pl.BlockSpec((1, LANES), lambda b, pt: (0, 0)),
                  pl.BlockSpec((LANES, LANES), lambda b, pt: (0, 0))],
        out_specs=pl.BlockSpec((1, PG_ROWS, LANES), lambda b, pt: (b, 0, 0)),
        scratch_shapes=[pltpu.VMEM((2, HEAD_DIM, PG_ROWS, LANES), F32), pltpu.SemaphoreType.DMA((2,))],
    )
    return pl.pallas_call(
        functools.partial(_compress_paged_kernel, norm=norm),
        grid_spec=grid_spec,
        out_shape=jax.ShapeDtypeStruct((DEC_BATCH, PG_ROWS, LANES), F32),
        compiler_params=_cparams(1),
        name="compress_paged",
    )(page_table, pool_t, pe_t, w1t, w2pair, gain2, s128)


def _masked_softmax_cols(s, valid):
    mx = jnp.max(s, axis=0, keepdims=True)
    e = jnp.where(valid, jnp.exp(s - mx), 0.0)
    den = jnp.sum(e, axis=0, keepdims=True)
    return e / jnp.maximum(den, 1e-30)


def _topk_members(imp, blk, n_blocks, row_of):
    def step(m, rank):
        row = row_of(m)
        beats = (row > imp) | ((row == imp) & (blk > m))
        return rank + jnp.where(beats, 1.0, 0.0)

    rank = jnp.zeros(imp.shape, F32)
    if isinstance(n_blocks, int) and n_blocks <= 32:
        for m in range(n_blocks):
            rank = step(m, rank)
    else:
        rank = lax.fori_loop(0, n_blocks, step, rank)
    return jnp.where(rank < float(TOPK), 1.0, 0.0)


def _attn_prompt_kernel(q_ref, kc_ref, vct_ref, ks_ref, vst_ref, kw_ref, vwt_ref, gt_ref,
                        bc_ref, bw_ref, bsn_ref, bsf_ref, e_ref, o_ref, am_ref):
    qb = pl.program_id(2)
    t0 = pl.multiple_of(qb * QT, QT)
    q = q_ref[0, 0].reshape(Q_ROWS, HEAD_DIM)

    def scores(k):
        return lax.dot_general(k, q, _NT, preferred_element_type=F32)

    def heads(x):
        return jnp.concatenate([x] * HPG, axis=1)

    bc = bc_ref[0, 0]
    pc = _masked_softmax_cols(scores(kc_ref[0, 0]) + bc, bc > 0.5 * NEG)
    o_c = jnp.dot(vct_ref[0, 0], pc.astype(BF16), preferred_element_type=F32)

    imp = pc[:, 0:QT] + pc[:, QT:2 * QT] + pc[:, 2 * QT:3 * QT] + pc[:, 3 * QT:4 * QT]
    blk = lax.broadcasted_iota(jnp.int32, (NB_P, QT), 0)
    cur = (t0 + lax.broadcasted_iota(jnp.int32, (NB_P, QT), 1)) // BLOCK
    forced = (blk == cur) | (blk == cur - 1) | (blk == 0)
    imp = imp + jnp.where(forced, FORCE, 0.0)
    imp = jnp.where(blk > cur, NEG, imp)
    sel = _topk_members(imp, blk, NB_P, lambda m: imp[m:m + 1, :]).astype(BF16)
    mexp = jnp.dot(e_ref[...], sel, preferred_element_type=F32)
    am_ref[...] = (mexp - 1.0) * (-NEG)

    span = WINDOW + QT
    kwin = kw_ref[0, 0, pl.ds(t0, span), :]
    vwin = vwt_ref[0, 0, :, pl.ds(t0, span)]
    sw = scores(kwin) + bw_ref[0]
    row = lax.broadcasted_iota(jnp.int32, (span, Q_ROWS), 0)
    sw = jnp.where(row >= WINDOW - t0, sw, NEG)
    mw = jnp.max(sw, axis=0, keepdims=True)
    pw = jnp.exp(sw - mw)
    lw = jnp.sum(pw, axis=0, keepdims=True)
    acc_w = jnp.dot(vwin, pw.astype(BF16), preferred_element_type=F32)

    kn = ks_ref[0, 0, pl.ds(t0, 2 * QT), :]
    vn = vst_ref[0, 0, :, pl.ds(t0, 2 * QT)]
    s = scores(kn) + bsn_ref[0] + heads(am_ref[pl.ds(t0, 2 * QT), :])
    m0 = jnp.max(s, axis=0, keepdims=True)
    p = jnp.exp(s - m0)
    l0 = jnp.sum(p, axis=0, keepdims=True)
    acc0 = jnp.dot(vn, p.astype(BF16), preferred_element_type=F32)
    bsf = bsf_ref[0]
    frow = lax.broadcasted_iota(jnp.int32, (FAR, Q_ROWS), 0)

    def far(c, carry):
        m, l, acc = carry
        off = pl.multiple_of(QT + c * FAR, QT)
        kf = ks_ref[0, 0, pl.ds(off, FAR), :]
        vf = vst_ref[0, 0, :, pl.ds(off, FAR)]
        sf = scores(kf) + bsf + heads(am_ref[pl.ds(off, FAR), :])
        sf = jnp.where(frow < t0 - off, sf, NEG)
        mn = jnp.maximum(m, jnp.max(sf, axis=0, keepdims=True))
        alpha = jnp.exp(m - mn)
        pf = jnp.exp(sf - mn)
        l = alpha * l + jnp.sum(pf, axis=0, keepdims=True)
        acc = alpha * acc + jnp.dot(vf, pf.astype(BF16), preferred_element_type=F32)
        return mn, l, acc

    n_far = (qb - 1 + FAR // QT - 1) // (FAR // QT)
    _, l_s, acc_s = lax.fori_loop(0, n_far, far, (m0, l0, acc0))

    gt = gt_ref[0, 0, 0]
    o = gt[0:1, :] * o_c + (gt[1:2, :] / l_s) * acc_s + (gt[2:3, :] / lw) * acc_w
    o_ref[...] = jnp.concatenate([o[:, h * QT:(h + 1) * QT].T for h in range(HPG)], axis=1).astype(BF16)


def _attn_prompt(q, kc, vct, ks, vst, kw, vwt, gt, bc, bw, bsn, bsf, e):
    tok = lambda n: pl.BlockSpec((1, 1, n, HEAD_DIM), lambda b, g, i: (b, g, 0, 0))
    feat = lambda n: pl.BlockSpec((1, 1, HEAD_DIM, n), lambda b, g, i: (b, g, 0, 0))
    return pl.pallas_call(
        _attn_prompt_kernel,
        grid=(BATCH, N_KV, N_QT),
        in_specs=[pl.BlockSpec((1, 1, HPG, QT, HEAD_DIM), lambda b, g, i: (b, g, 0, i, 0)),
                  tok(NB_P), feat(NB_P),
                  tok(QT + SEQ), feat(QT + SEQ),
                  tok(WINDOW + SEQ), feat(WINDOW + SEQ),
                  pl.BlockSpec((1, 1, 1, 3, Q_ROWS), lambda b, g, i: (b, g, i, 0, 0)),
                  pl.BlockSpec((1, 1, NB_P, Q_ROWS), lambda b, g, i: (g, i, 0, 0)),
                  pl.BlockSpec((1, WINDOW + QT, Q_ROWS), lambda b, g, i: (g, 0, 0)),
                  pl.BlockSpec((1, 2 * QT, Q_ROWS), lambda b, g, i: (g, 0, 0)),
                  pl.BlockSpec((1, 1, Q_ROWS), lambda b, g, i: (g, 0, 0)),
                  pl.BlockSpec((QT + SEQ, NB_P), lambda b, g, i: (0, 0))],
        out_specs=pl.BlockSpec((QT, HPG * HEAD_DIM), lambda b, g, i: (b * N_QT + i, g)),
        out_shape=jax.ShapeDtypeStruct((BATCH * SEQ, D_ATTN), BF16),
        scratch_shapes=[pltpu.VMEM((QT + SEQ, QT), F32)],
        compiler_params=_cparams(3),
        name="attn_prompt",
    )(q, kc, vct, ks, vst, kw, vwt, gt, bc, bw, bsn, bsf, e)


S_ROWS = N_KV * DEC_SEQ * HPG
S_CHUNKS = 4
S_CHUNK_PAGES = N_PAGES // S_CHUNKS
S_CHUNK = S_CHUNK_PAGES * PAGE_SIZE


def _group_diag(x):
    rg = S_ROWS // N_KV
    return jnp.concatenate([x[g * rg:(g + 1) * rg, g * HEAD_DIM:(g + 1) * HEAD_DIM] for g in range(N_KV)], axis=0)


def _attn_sample_kernel(pt_ref, q_ref, kc_ref, vc_ref, bcs_ref, sk_ref, sv_ref, kn_ref, vn_ref,
                        cf_ref, bsn_ref, e_ref, wk_ref, wv_ref, kwn_ref, vwn_ref, bws_ref, g_ref,
                        o_ref, kbuf, vbuf, s_ref, imp_ref, semk, semv):
    b = pl.program_id(0)
    kcps, vcps = [], []
    for p in range(N_PAGES):
        page = pt_ref[b, p]
        keys = pl.ds(p * PAGE_SIZE, PAGE_SIZE)
        c = p // S_CHUNK_PAGES
        kcps.append(pltpu.make_async_copy(sk_ref.at[page], kbuf.at[:, keys], semk.at[c]))
        vcps.append(pltpu.make_async_copy(sv_ref.at[page], vbuf.at[:, keys], semv.at[c]))
    for cp in kcps:
        cp.start()
    for cp in vcps:
        cp.start()

    q = q_ref[0]

    bcs = bcs_ref[...]
    st = lax.dot_general(kc_ref[0], q, _NT, preferred_element_type=F32) + bcs
    pc = _masked_softmax_cols(st, bcs > 0.5 * NEG)
    o_c = _group_diag(lax.dot_general(pc.astype(BF16), vc_ref[0], _TN, preferred_element_type=F32))

    lane = lax.broadcasted_iota(jnp.int32, (NB_S_PAD, S_ROWS), 1)
    s1 = pc + jnp.where((lane & 1) == 0, pltpu.roll(pc, S_ROWS - 1, axis=1), pltpu.roll(pc, 1, axis=1))
    imp = s1 + jnp.where((lane & 2) == 0, pltpu.roll(s1, S_ROWS - 2, axis=1), pltpu.roll(s1, 2, axis=1))
    blk = lax.broadcasted_iota(jnp.int32, (NB_S_PAD, S_ROWS), 0)
    cur = PAST_LEN // BLOCK
    forced = (blk == cur) | (blk == cur - 1) | (blk == 0)
    imp = imp + jnp.where(forced, FORCE, 0.0)
    imp = jnp.where(blk > cur, NEG, imp)
    imp_ref[...] = imp
    sel = _topk_members(imp, blk, jnp.int32(NB_S), lambda m: imp_ref[pl.ds(m, 1), :])
    sel = sel.astype(BF16)[0:NB_PAST]

    cf = cf_ref[...]
    near = bsn_ref[...]
    mx = None
    for c in range(S_CHUNKS):
        for cp in kcps[c * S_CHUNK_PAGES:(c + 1) * S_CHUNK_PAGES]:
            cp.wait()
        cols = slice(c * S_CHUNK, (c + 1) * S_CHUNK)
        s = jnp.dot(q, kbuf[:, cols].astype(BF16), preferred_element_type=F32)
        am = (lax.dot_general(sel, e_ref[:, cols], _TN, preferred_element_type=F32) - 1.0) * (-NEG)
        tiles = [cf] * (S_CHUNK // LANES)
        if c == S_CHUNKS - 1:
            tiles[-1] = near[:, 0:LANES]
        s = s + jnp.concatenate(tiles, axis=1) + am
        s_ref[:, cols] = s
        cm = jnp.max(s, axis=1, keepdims=True)
        mx = cm if mx is None else jnp.maximum(mx, cm)
    s_new = lax.dot_general(q, kn_ref[0], _NT, preferred_element_type=F32) + near[:, LANES:2 * LANES]
    mx = jnp.maximum(mx, jnp.max(s_new, axis=1, keepdims=True))
    p_new = jnp.exp(s_new - mx)
    l = jnp.sum(p_new, axis=1, keepdims=True)
    acc = jnp.dot(p_new.astype(BF16), vn_ref[0], preferred_element_type=F32)
    for c in range(S_CHUNKS):
        for cp in vcps[c * S_CHUNK_PAGES:(c + 1) * S_CHUNK_PAGES]:
            cp.wait()
        cols = slice(c * S_CHUNK, (c + 1) * S_CHUNK)
        p = jnp.exp(s_ref[:, cols] - mx)
        l = l + jnp.sum(p, axis=1, keepdims=True)
        acc = acc + lax.dot_general(p.astype(BF16), vbuf[:, cols].astype(BF16), _NT, preferred_element_type=F32)
    o_s = _group_diag(acc / l)

    bws = bws_ref[...]
    sw = jnp.dot(q, wk_ref[0].astype(BF16), preferred_element_type=F32) + bws[:, 0:WINDOW]
    swn = lax.dot_general(q, kwn_ref[0], _NT, preferred_element_type=F32) + bws[:, WINDOW:WINDOW + LANES]
    mw = jnp.maximum(jnp.max(sw, axis=1, keepdims=True), jnp.max(swn, axis=1, keepdims=True))
    pw = jnp.exp(sw - mw)
    pwn = jnp.exp(swn - mw)
    lw = jnp.sum(pw, axis=1, keepdims=True) + jnp.sum(pwn, axis=1, keepdims=True)
    accw = lax.dot_general(pw.astype(BF16), wv_ref[0].astype(BF16), _NT, preferred_element_type=F32)
    accw = accw + jnp.dot(pwn.astype(BF16), vwn_ref[0], preferred_element_type=F32)
    o_w = _group_diag(accw / lw)

    o_ref[0] = g_ref[0, 0] * o_c + g_ref[0, 1] * o_s + g_ref[0, 2] * o_w


def _attn_sample(page_table, qblk, kc, vc, bcs, pool_k, pool_v, kn, vn, cf, bsn, e, wk, wv, kwn, vwn, bws, gts):
    per_b = lambda *shape: pl.BlockSpec((1,) + shape, lambda b, pt: (b,) + (0,) * len(shape))
    const = lambda *shape: pl.BlockSpec(shape, lambda b, pt: (0,) * len(shape))
    grid_spec = pltpu.PrefetchScalarGridSpec(
        num_scalar_prefetch=1,
        grid=(DEC_BATCH,),
        in_specs=[per_b(S_ROWS, D_KV), per_b(NB_S_PAD, D_KV), per_b(NB_S_PAD, D_KV), const(NB_S_PAD, S_ROWS),
                  pl.BlockSpec(memory_space=pl.ANY), pl.BlockSpec(memory_space=pl.ANY),
                  per_b(LANES, D_KV), per_b(LANES, D_KV),
                  const(S_ROWS, LANES), const(S_ROWS, 2 * LANES), const(NB_PAST, PAST_LEN),
                  per_b(D_KV, WINDOW), per_b(D_KV, WINDOW), per_b(LANES, D_KV), per_b(LANES, D_KV),
                  const(S_ROWS, WINDOW + LANES), per_b(3, S_ROWS, HEAD_DIM)],
        out_specs=per_b(S_ROWS, HEAD_DIM),
        scratch_shapes=[pltpu.VMEM((D_KV, PAST_LEN), F32), pltpu.VMEM((D_KV, PAST_LEN), F32),
                        pltpu.VMEM((S_ROWS, PAST_LEN), F32), pltpu.VMEM((NB_S_PAD, S_ROWS), F32),
                        pltpu.SemaphoreType.DMA((S_CHUNKS,)), pltpu.SemaphoreType.DMA((S_CHUNKS,))],
    )
    return pl.pallas_call(
        _attn_sample_kernel,
        grid_spec=grid_spec,
        out_shape=jax.ShapeDtypeStruct((DEC_BATCH, S_ROWS, HEAD_DIM), F32),
        compiler_params=_cparams(1),
        name="attn_sample",
    )(page_table, qblk, kc, vc, bcs, pool_k, pool_v, kn, vn, cf, bsn, e, wk, wv, kwn, vwn, bws, gts)


def _sconv_kernel(bg_ref, cg_ref, u_ref, hcg_ref, hu_ref, init_ref, w_ref, y_ref, st_ref, ext_ref, *, tc):
    c = pl.program_id(1)

    @pl.when(c == 0)
    def _():
        ext_ref[0:8, :] = init_ref[0]

    @pl.when(c > 0)
    def _():
        ext_ref[0:8, :] = hcg_ref[...] * hu_ref[...]

    p = cg_ref[...] * u_ref[...]
    ext_ref[8:8 + tc, :] = p
    w = w_ref[...]
    y = ext_ref[pl.ds(6, tc), :] * w[0:1, :]
    y = y + ext_ref[pl.ds(7, tc), :] * w[1:2, :]
    y = y + ext_ref[pl.ds(8, tc), :] * w[2:3, :]
    y_ref[...] = (bg_ref[...] * y).astype(BF16)
    st_ref[0] = p[tc - 8:tc, :]


def _sconv(z, init, w, nb, t, tc):
    nch = t // tc
    tb = tc // 8
    row = lambda b, c: b * nch + c
    halo = lambda col: pl.BlockSpec((8, D_CONV), lambda b, c: (jnp.maximum(row(b, c) * tb - 1, 0), col))
    return pl.pallas_call(
        functools.partial(_sconv_kernel, tc=tc),
        grid=(nb, nch),
        in_specs=[pl.BlockSpec((tc, D_CONV), lambda b, c: (row(b, c), 0)),
                  pl.BlockSpec((tc, D_CONV), lambda b, c: (row(b, c), 1)),
                  pl.BlockSpec((tc, D_CONV), lambda b, c: (row(b, c), 2)),
                  halo(1), halo(2),
                  pl.BlockSpec((1, 8, D_CONV), lambda b, c: (b, 0, 0)),
                  pl.BlockSpec((3, D_CONV), lambda b, c: (0, 0))],
        out_specs=[pl.BlockSpec((tc, D_CONV), lambda b, c: (row(b, c), 0)),
                   pl.BlockSpec((1, 8, D_CONV), lambda b, c: (b, 0, 0))],
        out_shape=[jax.ShapeDtypeStruct((nb * t, D_CONV), BF16),
                   jax.ShapeDtypeStruct((nb, 8, D_CONV), F32)],
        scratch_shapes=[pltpu.VMEM((tc + 8, D_CONV), F32)],
        compiler_params=_cparams(2),
        name="sconv",
    )(z, z, z, z, z, init, w)


def _bucket_np(dist):
    n = np.maximum(dist, 0)
    max_exact = N_BUCKETS // 2
    nf = np.maximum(n, 1).astype(np.float32)
    scaled = np.log(nf / np.float32(max_exact)) / np.float32(math.log(MAX_DIST / max_exact)) * np.float32(N_BUCKETS - max_exact)
    large = np.minimum(max_exact + scaled.astype(np.int32), N_BUCKETS - 1)
    return np.where(n < max_exact, n, large).astype(np.int32)


def _dist_table(rel_bias):
    idx = _bucket_np(np.arange(MAX_DIST + 1))
    onehot = (idx[:, None] == np.arange(N_BUCKETS)[None, :]).astype(np.float32)
    return jnp.sum(onehot[None, :, :] * rel_bias.T.astype(F32)[:, None, :], axis=-1)


def _ramp(tabd, k_lo, k_hi, k_max=None):
    h = tabd.shape[0]
    k_max = k_hi if k_max is None else k_max
    parts = []
    n_neg = min(max(0, -k_lo), k_hi - k_lo)
    if n_neg:
        parts.append(jnp.full((h, n_neg), NEG, F32))
    lo = max(k_lo, 0)
    hi = min(k_hi, MAX_DIST + 1, k_max + 1)
    if hi > lo:
        parts.append(tabd[:, lo:hi])
    lo2 = max(lo, hi)
    hi2 = min(k_hi, k_max + 1)
    if hi2 > lo2:
        parts.append(jnp.broadcast_to(tabd[:, MAX_DIST:MAX_DIST + 1], (h, hi2 - lo2)))
    n_tail = k_hi - max(k_lo, k_max + 1)
    if n_tail > 0:
        parts.append(jnp.full((h, n_tail), NEG, F32))
    return jnp.concatenate(parts, axis=1)


def _toeplitz(r, n_rows, off, width):
    h, l0 = r.shape
    ln = max(l0, off + width) + 1
    r_ext = jnp.pad(r, ((0, 0), (0, ln - l0)))
    rows = jnp.tile(r_ext, (1, n_rows))[:, :n_rows * (ln - 1)].reshape(h, n_rows, ln - 1)
    return rows[:, :, off:off + width]


def _by_group(x):
    n = x.shape[1]
    return x.reshape(N_KV, HPG, n, QT).transpose(0, 2, 1, 3).reshape(N_KV, n, Q_ROWS)


def _prompt_tables(rel_bias):
    tabd = _dist_table(rel_bias)
    ks = np.arange(-2, 4)
    pat = jnp.stack([_ramp(tabd, 64 * k - 63, 64 * k - 63 + QT) for k in ks], axis=1)
    kk = np.clip(2 * np.arange(N_QT)[:, None] - np.arange(NB_P)[None, :], ks[0], ks[-1]) - ks[0]
    pick = (kk[:, :, None] == np.arange(len(ks))[None, None, :]).astype(np.float32)
    bc = jnp.sum(pick[None, :, :, :, None] * pat[:, None, None, :, :], axis=3)
    bc = bc.reshape(N_KV, HPG, N_QT, NB_P, QT).transpose(0, 2, 3, 1, 4).reshape(N_KV, N_QT, NB_P, Q_ROWS)
    span = WINDOW + QT
    bw = _by_group(_toeplitz(_ramp(tabd, 1 - QT, span, WINDOW), span, span - 1, QT))
    bsn = _by_group(_toeplitz(_ramp(tabd, 1 - QT, 2 * QT), 2 * QT, 2 * QT - 1, QT))
    bsf = jnp.broadcast_to(tabd[:, MAX_DIST].reshape(N_KV, 1, HPG, 1), (N_KV, 1, HPG, QT)).reshape(N_KV, 1, Q_ROWS)
    key_blk = np.concatenate([np.full((QT,), -1), np.arange(SEQ) // BLOCK])
    e = (key_blk[:, None] == np.arange(NB_P)[None, :]).astype(np.float32)
    return bc, bw, bsn, bsf, jnp.asarray(e, BF16)


def _sample_rows(x):
    n = x.shape[2]
    return x.reshape(N_KV, HPG, DEC_SEQ, n).transpose(0, 2, 1, 3).reshape(S_ROWS, n)


def _sample_tables(rel_bias):
    tabd = _dist_table(rel_bias)
    t_hi = DEC_SEQ - 1

    def by_token(k_of_col0, width, k_max=None):
        r = _ramp(tabd, k_of_col0 - width + 1, k_of_col0 + t_hi + 1, k_max)[:, ::-1]
        return _toeplitz(r, DEC_SEQ, t_hi, width)

    new = jnp.pad(by_token(0, DEC_SEQ), ((0, 0), (0, 0), (0, LANES - DEC_SEQ)), constant_values=NEG)
    last = PAST_LEN - 1
    far = jnp.broadcast_to(tabd[:, MAX_DIST:MAX_DIST + 1, None], (N_HEADS, DEC_SEQ, NB_PAST - 2))
    near = jnp.stack([_ramp(tabd, last - (BLOCK * m + BLOCK - 1) + 1, last - (BLOCK * m + BLOCK - 1) + 1 + DEC_SEQ)
                      for m in (NB_PAST - 2, NB_PAST - 1)], axis=2)
    bcs = jnp.concatenate([far, near, jnp.full((N_HEADS, DEC_SEQ, NB_S_PAD - NB_PAST), NEG, F32)], axis=2)
    bcs = _sample_rows(bcs).T
    cf = jnp.broadcast_to(_sample_rows(jnp.broadcast_to(tabd[:, MAX_DIST:MAX_DIST + 1, None], (N_HEADS, DEC_SEQ, 1))), (S_ROWS, LANES))
    bsn = _sample_rows(jnp.concatenate([by_token(LANES, LANES), new], axis=2))
    bws = _sample_rows(jnp.concatenate([by_token(WINDOW, WINDOW, WINDOW), new], axis=2))
    key_blk = np.arange(PAST_LEN) // BLOCK
    e = (key_blk[None, :] == np.arange(NB_PAST)[:, None]).astype(np.float32)
    return bcs, cf, bsn, bws, jnp.asarray(e, BF16)


def _feature_major(c):
    return c.transpose(0, 2, 3, 1).reshape(c.shape[0], D_KV, c.shape[1])


def _block_diag(w, n):
    a, b = w.shape
    return (jnp.eye(n, dtype=w.dtype)[:, None, :, None] * w[None, :, None, :]).reshape(n * a, n * b)


def _compress_weights(pe, w1, w2):
    pe_rows = pe.reshape(BLOCK, D_KV)
    eye = jnp.eye(N_KV, dtype=F32)
    w1big = (w1[:, None, :, None, :] * eye[None, :, None, :, None]).reshape(BLOCK * D_KV, D_KV).astype(BF16)
    w2big = _block_diag(w2, N_KV).astype(BF16)
    eye2 = jnp.eye(2, dtype=F32)
    w1t = (w1.transpose(1, 0, 2)[:, None, :, None, :] * eye2[None, :, None, :, None]).reshape(HEAD_DIM, LANES, LANES).astype(BF16)
    pe_d = jnp.tile(pe.transpose(2, 1, 0), (1, 2, 2))
    w2pair = _block_diag(w2, 2).astype(BF16)
    return pe_rows, w1big, w2big, pe_d, w1t, w2pair


def _layer0(y_p, y_s, norm_g, w_in, w_out, rel_bias, page_table, cache_cmp_k, cache_cmp_v, cache_sel_k,
            cache_sel_v, cache_win_k, cache_win_v, state_h, state_conv, rg_conv_w, rg_conv_b, rg_wa, rg_ba,
            rg_wx, rg_bx, rg_lambda, q_norm, k_norm, pe_k, pe_v, w1_k, w2_k, w1_v, w2_v):
    w_in_b = jnp.pad(w_in, ((0, 0), (0, D_IN_E_PAD - D_IN_E))).astype(BF16)
    w_out_b = w_out.astype(BF16)
    z_p = _norm_matmul(y_p, norm_g, w_in_b, 512, D_IN_E_PAD // 2)
    z_s = _norm_matmul(y_s, norm_g, w_in_b, 256, D_IN_E_PAD // 2)

    half = np.arange(LANES) // HEAD_DIM
    s128 = jnp.asarray(half[:, None] == half[None, :], BF16)
    q_gain = jnp.tile(q_norm, N_HEADS).reshape(1, D_ATTN)
    kc_gain = jnp.tile(k_norm[0], N_KV).reshape(1, D_KV)
    ks_gain = jnp.tile(k_norm[1], N_KV).reshape(1, D_KV)
    kw_gain = jnp.tile(k_norm[2], N_KV).reshape(1, D_KV)
    qn_p, ksn_p, kwn_p, gt_p = _prep0(z_p, q_gain, ks_gain, kw_gain, s128, 512)
    qn_s, ksn_s, kwn_s, gt_s = _prep0(z_s, q_gain, ks_gain, kw_gain, s128, 256)

    cw = rg_conv_w
    cb = rg_conv_b.reshape(1, D_RNN)
    wa = rg_wa.astype(BF16)
    wx = rg_wx.astype(BF16)
    ba = rg_ba.reshape(1, D_RNN)
    bx = rg_bx.reshape(1, D_RNN)
    lam = rg_lambda.reshape(1, D_RNN)
    yr_p, hl_p = _rglru(z_p, jnp.zeros((BATCH, 8, D_RNN), F32), jnp.zeros((BATCH, 1, D_RNN), F32),
                        cw, cb, wa, ba, wx, bx, lam, BATCH, SEQ, 256)
    cinit_s = jnp.pad(state_conv, ((0, 0), (5, 0), (0, 0)))
    yr_s, hl_s = _rglru(z_s, cinit_s, state_h.reshape(DEC_BATCH, 1, D_RNN),
                        cw, cb, wa, ba, wx, bx, lam, DEC_BATCH, DEC_SEQ, DEC_SEQ)

    col = lambda z, k: z[:, 3072 + 256 * k:3328 + 256 * k]
    kc_p, vc_p, vs_p, vw_p = col(z_p, 0), col(z_p, 1), col(z_p, 3), col(z_p, 5)
    kc_s, vc_s, vs_s, vw_s = col(z_s, 0), col(z_s, 1), col(z_s, 3), col(z_s, 5)

    pe_k_rows, w1big_k, w2big_k, pe_k_d, w1t_k, w2pair_k = _compress_weights(pe_k, w1_k, w2_k)
    pe_v_rows, w1big_v, w2big_v, pe_v_d, w1t_v, w2pair_v = _compress_weights(pe_v, w1_v, w2_v)
    n_new = DEC_SEQ * D_KV

    def dense_rows(c_p, c_s):
        tail = jnp.pad(c_s.reshape(DEC_BATCH, n_new), ((0, 0), (0, BLOCK * D_KV - n_new)))
        return jnp.concatenate([c_p.reshape(BATCH * NB_P, BLOCK * D_KV), tail], axis=0)

    kcb_d = _compress_dense(dense_rows(kc_p, kc_s), pe_k_rows.reshape(1, -1), w1big_k, w2big_k, kc_gain, s128, True)
    vcb_d = _compress_dense(dense_rows(vc_p, vc_s), pe_v_rows.reshape(1, -1), w1big_v, w2big_v, kc_gain, s128, False)
    gain2 = jnp.tile(k_norm[0], 2).reshape(1, LANES)
    kcb_pg = _compress_paged(page_table, cache_cmp_k, pe_k_d, w1t_k, w2pair_k, gain2, s128, True)
    vcb_pg = _compress_paged(page_table, cache_cmp_v, pe_v_d, w1t_v, w2pair_v, gain2, s128, False)
    n_pb = BATCH * NB_P

    def sample_blocks(paged, dense):
        paged = paged.reshape(DEC_BATCH, N_PAGES, N_KV, 2, HEAD_DIM).transpose(0, 1, 3, 2, 4)
        paged = paged.reshape(DEC_BATCH, NB_PAST, D_KV)
        x = jnp.concatenate([paged, dense[n_pb:].reshape(DEC_BATCH, 1, D_KV)], axis=1)
        return jnp.pad(x, ((0, 0), (0, NB_S_PAD - NB_S), (0, 0))).astype(BF16)

    kcb_s, vcb_s = sample_blocks(kcb_pg, kcb_d), sample_blocks(vcb_pg, vcb_d)

    def tok_major(x, pad):
        x = x.astype(BF16).reshape(BATCH, -1, N_KV, HEAD_DIM).transpose(0, 2, 1, 3)
        return jnp.pad(x, ((0, 0), (0, 0), (pad, 0), (0, 0)))

    def feat_major(x, pad):
        x = x.astype(BF16).reshape(BATCH, -1, N_KV, HEAD_DIM).transpose(0, 2, 3, 1)
        return jnp.pad(x, ((0, 0), (0, 0), (0, 0), (pad, 0)))

    q_r = qn_p.reshape(BATCH, SEQ, N_KV, HPG, HEAD_DIM).transpose(0, 2, 3, 1, 4)
    gt_r = gt_p[:, :3 * N_HEADS].reshape(BATCH, N_QT, QT, 3, N_KV, HPG).transpose(0, 4, 1, 3, 5, 2)
    gt_r = gt_r.reshape(BATCH, N_KV, N_QT, 3, Q_ROWS)
    bc, bw, bsn, bsf, e_p = _prompt_tables(rel_bias)
    o_p = _attn_prompt(q_r, tok_major(kcb_d[:n_pb], 0), feat_major(vcb_d[:n_pb], 0),
                       tok_major(ksn_p, QT), feat_major(vs_p, QT),
                       tok_major(kwn_p, WINDOW), feat_major(vw_p, WINDOW),
                       gt_r, bc, bw, bsn, bsf, e_p)

    q_g = qn_s.reshape(DEC_BATCH, DEC_SEQ, N_KV, HPG, HEAD_DIM).transpose(0, 2, 1, 3, 4)
    eye = jnp.eye(N_KV, dtype=BF16)
    qblk = (q_g[:, :, :, :, None, :] * eye[None, :, None, None, :, None]).reshape(DEC_BATCH, S_ROWS, D_KV)
    gts = gt_s[:, :3 * N_HEADS].reshape(DEC_BATCH, DEC_SEQ, 3, N_KV, HPG).transpose(0, 2, 3, 1, 4)
    gts = jnp.broadcast_to(gts.reshape(DEC_BATCH, 3, S_ROWS, 1), (DEC_BATCH, 3, S_ROWS, HEAD_DIM))
    bcs, cf, bsn_s, bws, e_s = _sample_tables(rel_bias)
    per_seq = lambda x: jnp.pad(x.astype(BF16).reshape(DEC_BATCH, DEC_SEQ, D_KV), ((0, 0), (0, LANES - DEC_SEQ), (0, 0)))
    o_sr = _attn_sample(page_table, qblk, kcb_s, vcb_s, bcs, cache_sel_k, cache_sel_v,
                        per_seq(ksn_s), per_seq(vs_s), cf, bsn_s, e_s,
                        _feature_major(cache_win_k), _feature_major(cache_win_v),
                        per_seq(kwn_s), per_seq(vw_s), bws, gts)
    o_s = o_sr.reshape(DEC_BATCH, N_KV, DEC_SEQ, HPG, HEAD_DIM).transpose(0, 2, 1, 3, 4).reshape(DEC_BATCH * DEC_SEQ, D_ATTN).astype(BF16)

    y_p = _proj_res(y_p, yr_p, 0, o_p, 0, w_out_b, 512)
    y_s = _proj_res(y_s, yr_s, 0, o_s, 0, w_out_b, 256)

    def kv_out(x, nb, t):
        return x.reshape(1, nb, t, N_KV, HEAD_DIM)

    win_k_s = jnp.concatenate([cache_win_k, kwn_s.reshape(DEC_BATCH, DEC_SEQ, N_KV, HEAD_DIM)], axis=1)[None, :, DEC_SEQ:]
    win_v_s = jnp.concatenate([cache_win_v, vw_s.reshape(DEC_BATCH, DEC_SEQ, N_KV, HEAD_DIM)], axis=1)[None, :, DEC_SEQ:]
    state = dict(
        cmp_k_p=kv_out(kc_p, BATCH, SEQ), cmp_k_s=kv_out(kc_s, DEC_BATCH, DEC_SEQ),
        cmp_v_p=kv_out(vc_p, BATCH, SEQ), cmp_v_s=kv_out(vc_s, DEC_BATCH, DEC_SEQ),
        sel_k_p=kv_out(ksn_p, BATCH, SEQ), sel_k_s=kv_out(ksn_s, DEC_BATCH, DEC_SEQ),
        sel_v_p=kv_out(vs_p, BATCH, SEQ), sel_v_s=kv_out(vs_s, DEC_BATCH, DEC_SEQ),
        win_k_p=kv_out(kwn_p, BATCH, SEQ)[:, :, SEQ - WINDOW:], win_k_s=win_k_s,
        win_v_p=kv_out(vw_p, BATCH, SEQ)[:, :, SEQ - WINDOW:], win_v_s=win_v_s,
        rglru_h_p=hl_p.reshape(1, BATCH, D_RNN), rglru_h_s=hl_s.reshape(1, DEC_BATCH, D_RNN),
        rglru_conv_p=z_p[:, :D_RNN].reshape(1, BATCH, SEQ, D_RNN)[:, :, SEQ - 3:],
        rglru_conv_s=z_s[:, :D_RNN].reshape(1, DEC_BATCH, DEC_SEQ, D_RNN)[:, :, DEC_SEQ - 3:],
    )
    return y_p, y_s, state


def _layer1(y_p, y_s, norm_g, w_in, sc_w, w_out, state_sconv):
    w_in_b = w_in.astype(BF16)
    w_out_b = w_out.astype(BF16)
    z_p = _norm_matmul(y_p, norm_g, w_in_b, 512, 2048)
    z_s = _norm_matmul(y_s, norm_g, w_in_b, 256, 2048)
    yb_p, st_p = _sconv(z_p, jnp.zeros((BATCH, 8, D_CONV), F32), sc_w, BATCH, SEQ, 256)
    yb_s, st_s = _sconv(z_s, jnp.pad(state_sconv, ((0, 0), (6, 0), (0, 0))), sc_w, DEC_BATCH, DEC_SEQ, DEC_SEQ)
    y_p = _proj_res(y_p, yb_p, 0, yb_p, 1, w_out_b, 512)
    y_s = _proj_res(y_s, yb_s, 0, yb_s, 1, w_out_b, 256)
    return y_p, y_s, st_p[None, :, 6:8], st_s[None, :, 6:8]


def kernel(x_prompt, x_sample, cache_cmp_k, cache_cmp_v, cache_sel_k, cache_sel_v, cache_win_k, cache_win_v, state_rglru_h, state_rglru_conv, state_sconv, page_table, rel_bias, norm_mix, norm_ffn, w_ff1, w_ff2, w_in_e, w_out_e, rg_conv_w, rg_conv_b, rg_wa, rg_ba, rg_wx, rg_bx, rg_lambda, q_norm, k_norm, cmp_pe_k, cmp_pe_v, cmp_w1_k, cmp_w2_k, cmp_w1_v, cmp_w2_v, w_in_o, sc_w, w_out_o):
    y_p = x_prompt.reshape(BATCH * SEQ, D_MODEL)
    y_s = x_sample.reshape(DEC_BATCH * DEC_SEQ, D_MODEL)
    pool = lambda c: _feature_major(c[0])

    y_p, y_s, st = _layer0(
        y_p, y_s, norm_mix[0], w_in_e[0], w_out_e[0], rel_bias, page_table,
        pool(cache_cmp_k), pool(cache_cmp_v), pool(cache_sel_k), pool(cache_sel_v), cache_win_k[0], cache_win_v[0],
        state_rglru_h[0], state_rglru_conv[0], rg_conv_w[0], rg_conv_b[0], rg_wa[0], rg_ba[0], rg_wx[0],
        rg_bx[0], rg_lambda[0], q_norm[0], k_norm[0], cmp_pe_k[0], cmp_pe_v[0], cmp_w1_k[0], cmp_w2_k[0],
        cmp_w1_v[0], cmp_w2_v[0])
    w1 = w_ff1.astype(BF16)
    w2 = w_ff2.astype(BF16)
    y_p = _ffn(y_p, norm_ffn[0], w1[0], w2[0], 512, 512)
    y_s = _ffn(y_s, norm_ffn[0], w1[0], w2[0], 256, 512)

    y_p, y_s, sconv_p, sconv_s = _layer1(y_p, y_s, norm_mix[1], w_in_o[0], sc_w[0], w_out_o[0], state_sconv[0])
    y_p = _ffn(y_p, norm_ffn[1], w1[1], w2[1], 512, 512)
    y_s = _ffn(y_s, norm_ffn[1], w1[1], w2[1], 256, 512)

    return (y_p.reshape(BATCH, SEQ, D_MODEL), y_s.reshape(DEC_BATCH, DEC_SEQ, D_MODEL),
            st["cmp_k_p"], st["cmp_k_s"], st["cmp_v_p"], st["cmp_v_s"],
            st["sel_k_p"], st["sel_k_s"], st["sel_v_p"], st["sel_v_s"],
            st["win_k_p"], st["win_k_s"], st["win_v_p"], st["win_v_s"],
            st["rglru_h_p"], st["rglru_h_s"], st["rglru_conv_p"], st["rglru_conv_s"],
            sconv_p, sconv_s)
```
